```python
import jax
import jax.numpy as jnp
from jax import lax
import numpy as np

D_MODEL = 1024
BATCH = 32
SEQ = 256
DEPTH = 4
DEC_BATCH = 4
DEC_SEQ = 1024
PAST_LEN = 256

GRID_W = 64
CHUNK = 64
ROPE_BASE = 10000.0
ALPHA = (2 * DEPTH) ** 0.25
BETA_INIT = (8 * DEPTH) ** -0.25
LN_EPS = 1e-6
RET_HEADS = 4
RET_DK = 64
RET_DV = 64
RET_WIDTH = RET_HEADS * RET_DV
S5_WIDTH = 256
S5_CH = 16
S5_GROUPS = S5_WIDTH // S5_CH
S5_P = 64
GDN_HEADS = 8
GDN_DK = 64
GDN_DV = 64
GDN_WIDTH = GDN_HEADS * GDN_DV
CONV_W = 3
MIX_WIDTH = RET_WIDTH + S5_WIDTH + GDN_WIDTH
IN_SIZES = (RET_HEADS * RET_DK, RET_HEADS * RET_DK, RET_WIDTH, RET_WIDTH, S5_WIDTH,
            GDN_HEADS * GDN_DK, GDN_HEADS * GDN_DK, GDN_WIDTH, GDN_WIDTH, GDN_HEADS, GDN_HEADS)
IN_COLS = sum(IN_SIZES)
IN_SPLITS = tuple(int(s) for s in np.cumsum(IN_SIZES)[:-1])
N_EGROUPS = 4
EXPERTS_PER_GROUP = 4
N_EXPERTS = N_EGROUPS * EXPERTS_PER_GROUP
TOP_K = 2
D_FF_EXPERT = 512

kernel_name = "hybrid_retention_s5_gdn_hmoe_diffusion_step"

F32 = jnp.float32


def layer_norm(x, gain=None, bias=None):
    xf = x.astype(F32)
    mu = jnp.mean(xf, -1, keepdims=True)
    var = jnp.mean(jnp.square(xf - mu), -1, keepdims=True)
    y = (xf - mu) * lax.rsqrt(var + LN_EPS)
    if gain is not None:
        y = y * gain.astype(F32) + bias.astype(F32)
    return y.astype(x.dtype)


def l2_normalize(x):
    return x * lax.rsqrt(jnp.sum(x * x, -1, keepdims=True) + 1e-6)


def grid_rope(length):
    rows = length // GRID_W
    row = jnp.repeat(jnp.arange(rows, dtype=F32), GRID_W)
    col = (jnp.arange(rows * GRID_W) % GRID_W).astype(F32)
    n_freq = RET_DK // 4
    inv_freq = ROPE_BASE ** (-jnp.arange(n_freq, dtype=F32) / n_freq)
    ang = jnp.concatenate([row[:, None] * inv_freq, col[:, None] * inv_freq], -1)
    return jnp.cos(ang), jnp.sin(ang)


def apply_rope(x, cos, sin):
    half = x.shape[-1] // 2
    x1, x2 = x[..., :half], x[..., half:]
    cos, sin = cos[None, :, None, :], sin[None, :, None, :]
    return jnp.concatenate([x1 * cos - x2 * sin, x1 * sin + x2 * cos], -1)


def depthwise_conv(x, w):
    return lax.conv_general_dilated(x, w[:, None, :].astype(x.dtype), window_strides=(1,), padding="SAME",
                                    dimension_numbers=("NWC", "WIO", "NWC"), feature_group_count=x.shape[-1])


def retention_dir(q, k, v, log_gamma, r0):
    b, h, n_tok, dk = q.shape
    n = n_tok // CHUNK
    qc = q.reshape(b, h, n, CHUNK, dk)
    kc = k.reshape(b, h, n, CHUNK, dk)
    vc = v.reshape(b, h, n, CHUNK, -1)
    pos = jnp.arange(CHUNK, dtype=F32)
    lg = log_gamma[:, None]
    diff = pos[:, None] - pos[None, :]
    dmat = jnp.where(diff >= 0, jnp.exp(lg[:, :, None] * jnp.maximum(diff, 0.0)), 0.0)
    scores = jnp.einsum("bhnid,bhnjd->bhnij", qc, kc) * dmat[None, :, None]
    o_intra = jnp.einsum("bhnij,bhnje->bhnie", scores, vc)
    k_end = kc * jnp.exp(lg * (CHUNK - 1 - pos))[None, :, None, :, None]
    kv = jnp.einsum("bhncd,bhnce->nbhde", k_end, vc)
    chunk_decay = jnp.exp(log_gamma * CHUNK)[None, :, None, None]

    def step(r, kv_n):
        return chunk_decay * r + kv_n, r

    r_fin, r_in = lax.scan(step, r0, kv)
    q_dec = qc * jnp.exp(lg * (pos + 1.0))[None, :, None, :, None]
    o_cross = jnp.einsum("bhncd,nbhde->bhnce", q_dec, r_in)
    return (o_intra + o_cross).reshape(b, h, n_tok, -1), r_fin


def gdn_dir(q, k, v, beta, g, s0):
    b, h, n_tok, dk = q.shape
    dv = v.shape[-1]
    n = n_tok // CHUNK
    qc = q.reshape(b, h, n, CHUNK, dk)
    kc = k.reshape(b, h, n, CHUNK, dk)
    vc = v.reshape(b, h, n, CHUNK, dv)
    bc = beta.reshape(b, h, n, CHUNK)
    gc = jnp.cumsum(g.reshape(b, h, n, CHUNK), -1)
    pos = jnp.arange(CHUNK)
    incl = pos[:, None] >= pos[None, :]
    strict = pos[:, None] > pos[None, :]
    diff = gc[..., :, None] - gc[..., None, :]
    dmat = jnp.where(incl, jnp.exp(jnp.where(incl, diff, 0.0)), 0.0)
    kb = kc * bc[..., None]
    m = jnp.where(strict, jnp.einsum("bhnid,bhnjd->bhnij", kb, kc) * dmat, 0.0)
    rhs = jnp.concatenate([kb * jnp.exp(gc)[..., None], vc * bc[..., None]], -1)
    sol = lax.linalg.triangular_solve(m + jnp.eye(CHUNK, dtype=m.dtype), rhs, left_side=True,
                                      lower=True, unit_diagonal=True)
    w, u = sol[..., :dk], sol[..., dk:]
    attn = jnp.where(incl, jnp.einsum("bhnid,bhnjd->bhnij", qc, kc) * dmat, 0.0)
    q_dec = qc * jnp.exp(gc)[..., None]
    k_dec = kc * jnp.exp(gc[..., -1:] - gc)[..., None]
    g_end = jnp.exp(gc[..., -1])[..., None, None]
    xs = tuple(jnp.moveaxis(a, 2, 0) for a in (w, u, attn, q_dec, k_dec, g_end))

    def step(s, inp):
        w_n, u_n, a_n, qd_n, kd_n, ge_n = inp
        v_new = u_n - jnp.einsum("bhck,bhkv->bhcv", w_n, s)
        o = jnp.einsum("bhck,bhkv->bhcv", qd_n, s) + jnp.einsum("bhij,bhjv->bhiv", a_n, v_new)
        s = s * ge_n + jnp.einsum("bhck,bhcv->bhkv", kd_n, v_new)
        return s, o

    s_fin, o = lax.scan(step, s0, xs)
    return jnp.moveaxis(o, 0, 2).reshape(b, h, n_tok, dv), s_fin


def affine_combine(e1, e2):
    a1, b1 = e1
    a2, b2 = e2
    return a1 * a2, a2 * b1 + b2


def s5_scan(bu, a_bar, x0):
    bu = bu.at[:, 0].add(a_bar * x0)
    a = jnp.broadcast_to(a_bar, bu.shape)
    _, xs = lax.associative_scan(affine_combine, (a, bu), axis=1)
    return xs, xs[:, -1]


def s5_mixer(u, lp, x0_re, x0_im):
    b, n_tok, _ = u.shape
    uf = u.astype(F32).reshape(b, n_tok, S5_GROUPS, S5_CH)
    lam = lax.complex(lp["s5_a_re"].astype(F32), lp["s5_a_im"].astype(F32))
    dt = jnp.exp(lp["s5_log_dt"].astype(F32))[..., None]
    a_bar = jnp.exp(lam * dt)
    b_mat = lax.complex(lp["s5_b_re"].astype(F32), lp["s5_b_im"].astype(F32))
    c_mat = lax.complex(lp["s5_c_re"].astype(F32), lp["s5_c_im"].astype(F32))
    b_bar = ((a_bar - 1.0) / lam)[..., None] * b_mat[None]
    x0 = lax.complex(x0_re.astype(F32), x0_im.astype(F32))
    uc = uf.astype(jnp.complex64)
    bu_f = jnp.einsum("blgc,gpc->blgp", uc, b_bar[0])
    bu_b = jnp.einsum("blgc,gpc->blgp", jnp.flip(uc, 1), b_bar[1])
    xs_f, fin_f = s5_scan(bu_f, a_bar[0], x0[:, 0])
    xs_b, fin_b = s5_scan(bu_b, a_bar[1], x0[:, 1])
    states = xs_f + jnp.flip(xs_b, 1)
    y = jnp.real(jnp.einsum("gcp,blgp->blgc", c_mat, states)) \
        + lp["s5_d"].astype(F32).reshape(S5_GROUPS, S5_CH) * uf
    y = jax.nn.gelu(y.reshape(b, n_tok, S5_WIDTH))
    y = y * jax.nn.sigmoid(y @ lp["s5_w_glu"].astype(F32))
    fin = jnp.stack([fin_f, fin_b], 1)
    return y, jnp.real(fin), jnp.imag(fin)


def token_mixers(h, lp, init, rope):
    b, n_tok, _ = h.shape
    ret0, s5re0, s5im0, gdn0 = init
    proj = h @ lp["w_in"]
    rq, rk, rv, rg, su, gq, gk, gv, gz, ga, gb = jnp.split(proj, IN_SPLITS, axis=-1)

    q = rq.astype(F32).reshape(b, n_tok, RET_HEADS, RET_DK)
    k = rk.astype(F32).reshape(b, n_tok, RET_HEADS, RET_DK) * RET_DK ** -0.5
    if rope is not None:
        q = apply_rope(q, *rope)
        k = apply_rope(k, *rope)
    q, k = q.transpose(0, 2, 1, 3), k.transpose(0, 2, 1, 3)
    v = rv.astype(F32).reshape(b, n_tok, RET_HEADS, RET_DV).transpose(0, 2, 1, 3)
    log_gamma = -jnp.exp(lp["ret_decay"].astype(F32))
    o_f, r_f = retention_dir(q, k, v, log_gamma[0], ret0[:, 0].astype(F32))
    o_b, r_b = retention_dir(jnp.flip(q, 2), jnp.flip(k, 2), jnp.flip(v, 2), log_gamma[1],
                             ret0[:, 1].astype(F32))
    o_ret = layer_norm(o_f + jnp.flip(o_b, 2))
    y_ret = jax.nn.silu(rg.astype(F32)) * o_ret.transpose(0, 2, 1, 3).reshape(b, n_tok, RET_WIDTH)

    y_s5, s5_re, s5_im = s5_mixer(su, lp, s5re0, s5im0)

    qkv = jax.nn.silu(depthwise_conv(jnp.concatenate([gq, gk, gv], -1), lp["gdn_conv"]))
    cq, ck, cv = jnp.split(qkv.astype(F32), 3, axis=-1)
    q = l2_normalize(cq.reshape(b, n_tok, GDN_HEADS, GDN_DK)) * GDN_DK ** -0.5
    k = l2_normalize(ck.reshape(b, n_tok, GDN_HEADS, GDN_DK))
    v = cv.reshape(b, n_tok, GDN_HEADS, GDN_DV)
    q, k, v = (t.transpose(0, 2, 1, 3) for t in (q, k, v))
    beta = jax.nn.sigmoid(gb.astype(F32)).transpose(0, 2, 1)
    rate = jnp.exp(lp["gdn_a_log"].astype(F32))
    dt_in = ga.astype(F32)
    dt_bias = lp["gdn_dt_bias"].astype(F32)
    g_f = (-rate[0] * jax.nn.softplus(dt_in + dt_bias[0])).transpose(0, 2, 1)
    g_b = (-rate[1] * jax.nn.softplus(dt_in + dt_bias[1])).transpose(0, 2, 1)
    o_f, s_f = gdn_dir(q, k, v, beta, g_f, gdn0[:, 0].astype(F32))
    o_b, s_b = gdn_dir(jnp.flip(q, 2), jnp.flip(k, 2), jnp.flip(v, 2), jnp.flip(beta, 2),
                       jnp.flip(g_b, 2), gdn0[:, 1].astype(F32))
    o = (o_f + jnp.flip(o_b, 2)).transpose(0, 2, 1, 3)
    o = o * lax.rsqrt(jnp.mean(o * o, -1, keepdims=True) + LN_EPS) * lp["gdn_norm"].astype(F32)
    y_gdn = (o * jax.nn.silu(gz.astype(F32).reshape(b, n_tok, GDN_HEADS, GDN_DV))).reshape(b, n_tok, GDN_WIDTH)

    mixed = jnp.concatenate([y_ret, y_s5, y_gdn], -1).astype(h.dtype) @ lp["w_out"]
    states = (jnp.stack([r_f, r_b], 1), s5_re, s5_im, jnp.stack([s_f, s_b], 1))
    return mixed, states


def hier_moe(h, lp):
    b, n_tok, d = h.shape
    t = h.reshape(b * n_tok, d)
    g_logits = (t @ lp["router_group_w"] + lp["router_group_b"]).astype(F32)
    g_prob = jax.nn.softmax(g_logits, -1)
    g_sel = jnp.argmax(g_logits, -1)
    p_group = jnp.take_along_axis(g_prob, g_sel[:, None], 1)
    e_logits = (t @ lp["router_expert_w"] + lp["router_expert_b"]).astype(F32)
    e_logits = e_logits.reshape(-1, N_EGROUPS, EXPERTS_PER_GROUP)
    e_logits = jnp.take_along_axis(e_logits, g_sel[:, None, None], 1)[:, 0]
    e_prob = jax.nn.softmax(e_logits, -1)
    top_p, top_i = lax.top_k(e_prob, TOP_K)
    weights = p_group * top_p / jnp.sum(top_p, -1, keepdims=True)
    expert_id = g_sel[:, None] * EXPERTS_PER_GROUP + top_i
    gate = jnp.einsum("tk,tke->te", weights, jax.nn.one_hot(expert_id, N_EXPERTS, dtype=F32)).astype(h.dtype)
    a = jnp.einsum("td,edf->tef", t, lp["w_gate"])
    u = jnp.einsum("td,edf->tef", t, lp["w_up"])
    y = jnp.einsum("tef,efd->td", jax.nn.silu(a) * u * gate[:, :, None], lp["w_down"])
    return y.reshape(b, n_tok, d)


def trunk_layer(x, cond, lp, init, rope):
    mod = jax.nn.silu(cond) @ lp["w_ada"] + lp["b_ada"]
    sh1, sc1, g1, sh2, sc2, g2 = jnp.split(mod[:, None, :], 6, axis=-1)
    h = layer_norm(x) * (1 + sc1) + sh1
    mix, states = token_mixers(h, lp, init, rope)
    x = layer_norm(ALPHA * x + g1 * mix, lp["ln1_g"], lp["ln1_b"])
    h = layer_norm(x) * (1 + sc2) + sh2
    x = layer_norm(ALPHA * x + g2 * hier_moe(h, lp), lp["ln2_g"], lp["ln2_b"])
    return x, states


def zero_states(b):
    return (jnp.zeros((b, 2, RET_HEADS, RET_DK, RET_DV), F32),
            jnp.zeros((b, 2, S5_GROUPS, S5_P), F32),
            jnp.zeros((b, 2, S5_GROUPS, S5_P), F32),
            jnp.zeros((b, 2, GDN_HEADS, GDN_DK, GDN_DV), F32))


def setup_inputs(seed: int = 0) -> dict:
    key = jax.random.key(seed)
    keys = jax.random.split(key, 48)
    counter = iter(range(48))

    def nk():
        return keys[next(counter)]

    def nrm(shape, scale):
        return scale * jax.random.normal(nk(), shape, F32)

    nl, d = DEPTH, D_MODEL
    ret_base = jnp.asarray(np.log(-np.log(1.0 - 2.0 ** (-5.0 - np.arange(RET_HEADS)))), F32)
    log_dt_lo, log_dt_hi = float(np.log(1e-3)), float(np.log(1e-1))
    dt_gdn = jnp.exp(jax.random.uniform(nk(), (nl, 2, GDN_HEADS), F32, minval=log_dt_lo, maxval=log_dt_hi))
    return {
        "x_prompt": nrm((BATCH, SEQ, d), 1.0),
        "x_sample": nrm((DEC_BATCH, DEC_SEQ, d), 1.0),
        "state_ret": nrm((DEC_BATCH, nl, 2, RET_HEADS, RET_DK, RET_DV), 0.2),
        "state_s5_re": nrm((DEC_BATCH, nl, 2, S5_GROUPS, S5_P), 0.1),
        "state_s5_im": nrm((DEC_BATCH, nl, 2, S5_GROUPS, S5_P), 0.1),
        "state_gdn": nrm((DEC_BATCH, nl, 2, GDN_HEADS, GDN_DK, GDN_DV), 0.1),
        "c": nrm((DEC_BATCH, d), 1.0),
        "c_ctx": nrm((d,), 1.0),
        "w_ada": nrm((nl, d, 6 * d), 0.5 * d ** -0.5),
        "b_ada": nrm((nl, 6 * d), 0.02),
        "w_in": nrm((nl, d, IN_COLS), d ** -0.5),
        "ret_decay": ret_base + nrm((nl, 2, RET_HEADS), 0.05),
        "s5_a_re": -0.5 * jnp.exp(nrm((nl, 2, S5_GROUPS, S5_P), 0.05)),
        "s5_a_im": jnp.pi * jnp.arange(S5_P, dtype=F32) + nrm((nl, 2, S5_GROUPS, S5_P), 0.01),
        "s5_log_dt": jax.random.uniform(nk(), (nl, 2, S5_GROUPS), F32, minval=log_dt_lo, maxval=log_dt_hi),
        "s5_b_re": nrm((nl, S5_GROUPS, S5_P, S5_CH), (2 * S5_CH) ** -0.5),
        "s5_b_im": nrm((nl, S5_GROUPS, S5_P, S5_CH), (2 * S5_CH) ** -0.5),
        "s5_c_re": nrm((nl, S5_GROUPS, S5_CH, S5_P), S5_P ** -0.5),
        "s5_c_im": nrm((nl, S5_GROUPS, S5_CH, S5_P), S5_P ** -0.5),
        "s5_d": nrm((nl, S5_WIDTH), 1.0),
        "s5_w_glu": nrm((nl, S5_WIDTH, S5_WIDTH), S5_WIDTH ** -0.5),
        "gdn_conv": nrm((nl, CONV_W, 3 * GDN_WIDTH), CONV_W ** -0.5),
        "gdn_a_log": jnp.log(jax.random.uniform(nk(), (nl, 2, GDN_HEADS), F32, minval=1.0, maxval=16.0)),
        "gdn_dt_bias": dt_gdn + jnp.log(-jnp.expm1(-dt_gdn)),
        "gdn_norm": 1.0 + nrm((nl, GDN_DV), 0.02),
        "w_out": nrm((nl, MIX_WIDTH, d), BETA_INIT * MIX_WIDTH ** -0.5),
        "ln1_g": 1.0 + nrm((nl, d), 0.02),
        "ln1_b": nrm((nl, d), 0.02),
        "ln2_g": 1.0 + nrm((nl, d), 0.02),
        "ln2_b": nrm((nl, d), 0.02),
        "router_group_w": nrm((nl, d, N_EGROUPS), d ** -0.5),
        "router_group_b": nrm((nl, N_EGROUPS), 0.01),
        "router_expert_w": nrm((nl, d, N_EXPERTS), d ** -0.5),
        "router_expert_b": nrm((nl, N_EXPERTS), 0.01),
        "w_gate": nrm((nl, N_EXPERTS, d, D_FF_EXPERT), d ** -0.5),
        "w_up": nrm((nl, N_EXPERTS, d, D_FF_EXPERT), d ** -0.5),
        "w_down": nrm((nl, N_EXPERTS, D_FF_EXPERT, d), BETA_INIT * D_FF_EXPERT ** -0.5),
    }


def reference(x_prompt, x_sample, state_ret, state_s5_re, state_s5_im, state_gdn, c, c_ctx,
              w_ada, b_ada, w_in, ret_decay, s5_a_re, s5_a_im, s5_log_dt, s5_b_re, s5_b_im,
              s5_c_re, s5_c_im, s5_d, s5_w_glu, gdn_conv, gdn_a_log, gdn_dt_bias, gdn_norm, w_out,
              ln1_g, ln1_b, ln2_g, ln2_b, router_group_w, router_group_b, router_expert_w,
              router_expert_b, w_gate, w_up, w_down):
    rope = grid_rope(x_sample.shape[1])
    ctx_init = zero_states(x_prompt.shape[0])
    cond_ctx = c_ctx[None, :]
    xp, xs = x_prompt, x_sample
    ret_list, s5re_list, s5im_list, gdn_list = [], [], [], []
    for l in range(DEPTH):
        lp = dict(w_ada=w_ada[l], b_ada=b_ada[l], w_in=w_in[l], ret_decay=ret_decay[l],
                  s5_a_re=s5_a_re[l], s5_a_im=s5_a_im[l], s5_log_dt=s5_log_dt[l],
                  s5_b_re=s5_b_re[l], s5_b_im=s5_b_im[l], s5_c_re=s5_c_re[l], s5_c_im=s5_c_im[l],
                  s5_d=s5_d[l], s5_w_glu=s5_w_glu[l], gdn_conv=gdn_conv[l], gdn_a_log=gdn_a_log[l],
                  gdn_dt_bias=gdn_dt_bias[l], gdn_norm=gdn_norm[l], w_out=w_out[l],
                  ln1_g=ln1_g[l], ln1_b=ln1_b[l], ln2_g=ln2_g[l], ln2_b=ln2_b[l],
                  router_group_w=router_group_w[l], router_group_b=router_group_b[l],
                  router_expert_w=router_expert_w[l], router_expert_b=router_expert_b[l],
                  w_gate=w_gate[l], w_up=w_up[l], w_down=w_down[l])
        xp, (r_s, sre_s, sim_s, g_s) = trunk_layer(xp, cond_ctx, lp, ctx_init, None)
        ret_list.append(r_s)
        s5re_list.append(sre_s)
        s5im_list.append(sim_s)
        gdn_list.append(g_s)
        lat_init = (state_ret[:, l], state_s5_re[:, l], state_s5_im[:, l], state_gdn[:, l])
        xs, _ = trunk_layer(xs, c, lp, lat_init, rope)
    new_state_ret = jnp.stack(ret_list, 1)
    new_state_s5_re = jnp.stack(s5re_list, 1)
    new_state_s5_im = jnp.stack(s5im_list, 1)
    new_state_gdn = jnp.stack(gdn_list, 1)
    return (xp, xs, new_state_ret, new_state_s5_re, new_state_s5_im, new_state_gdn)
```

```python
import functools

import jax
import jax.numpy as jnp
import numpy as np
from jax import lax
from jax.experimental import pallas as pl
from jax.experimental.pallas import tpu as pltpu

F32 = jnp.float32
BF16 = jnp.bfloat16
HIGHEST = lax.Precision.HIGHEST

D_MODEL = 1024
DEPTH = 4
N_CTX, L_CTX = 32, 256
N_LAT, L_LAT = 4, 1024
T_CTX = N_CTX * L_CTX
T_LAT = N_LAT * L_LAT
T_ALL = T_CTX + T_LAT
GRID_W = 64
CHUNK = 64
ROPE_BASE = 10000.0
ALPHA = (2 * DEPTH) ** 0.25
LN_EPS = 1e-6
RET_HEADS, RET_DK = 4, 64
RET_WIDTH = 256
S5_WIDTH, S5_CH, S5_GROUPS, S5_P = 256, 16, 16, 64
S5_STATE = S5_GROUPS * S5_P
GDN_HEADS, GDN_DK = 8, 64
GDN_WIDTH = 512
N_EXPERTS, EXPERTS_PER_GROUP, N_EGROUPS = 16, 4, 4
D_FF = 512
MAIN_COLS = 3328
LANES = 128
TOKEN_TILE = 512
N_TILES = T_ALL // TOKEN_TILE
CTX_TILES = T_CTX // TOKEN_TILE
TILES_PER_LAT = L_LAT // TOKEN_TILE
CTX_MOD_ROW = N_LAT
VMEM_LIMIT = 56 * 1024 * 1024


def _params(*sem):
    return pltpu.CompilerParams(dimension_semantics=sem, vmem_limit_bytes=VMEM_LIMIT)


def _ln(x):
    mu = jnp.mean(x, -1, keepdims=True)
    xc = x - mu
    var = jnp.mean(xc * xc, -1, keepdims=True)
    return xc * lax.rsqrt(var + LN_EPS)


def _silu(x):
    return x * jax.nn.sigmoid(x)


def _dot(a, b):
    return jnp.dot(a, b, preferred_element_type=F32)


def _dot_f32(a, b):
    return jnp.dot(a, b, precision=HIGHEST, preferred_element_type=F32)


def _dot_nt(a, b):
    return lax.dot_general(a, b, (((1,), (1,)), ((), ())), preferred_element_type=F32)


def _dot_tn(a, b):
    return lax.dot_general(a, b, (((0,), (0,)), ((), ())), preferred_element_type=F32)


def _mod_row(i):
    return jnp.where(i < CTX_TILES, CTX_MOD_ROW, (i - CTX_TILES) // TILES_PER_LAT)


ADA_TN = 1536


def _ada_kernel(c_ref, w_ref, b_ref, o_ref):
    o_ref[...] = _dot_f32(_silu(c_ref[...]), w_ref[...]) + b_ref[...]


def _ada_table(cond8, w_ada, b_ada):
    return pl.pallas_call(
        _ada_kernel,
        grid=(DEPTH, 6 * D_MODEL // ADA_TN),
        in_specs=[
            pl.BlockSpec((8, D_MODEL), lambda l, j: (0, 0)),
            pl.BlockSpec((None, D_MODEL, ADA_TN), lambda l, j: (l, 0, j)),
            pl.BlockSpec((None, 1, ADA_TN), lambda l, j: (l, 0, j)),
        ],
        out_specs=pl.BlockSpec((None, 8, ADA_TN), lambda l, j: (l, 0, j)),
        out_shape=jax.ShapeDtypeStruct((DEPTH, 8, 6 * D_MODEL), F32),
        compiler_params=_params("arbitrary", "arbitrary"),
        name="ada_table",
    )(cond8, w_ada, b_ada.reshape(DEPTH, 1, 6 * D_MODEL))


def _inproj_kernel(x_ref, mod_ref, w_ref, wab_ref, ret_ref, s5_ref, gdn_ref, ab_ref):
    mod = mod_ref[...]
    sh1 = mod[:, 0:D_MODEL]
    sc1 = mod[:, D_MODEL:2 * D_MODEL]
    h = (_ln(x_ref[...]) * (1.0 + sc1) + sh1).astype(BF16)
    ret_ref[...] = _dot(h, w_ref[:, 0:1024])
    s5_ref[...] = _dot(h, w_ref[:, 1024:1280])
    gdn_ref[...] = _dot(h, w_ref[:, 1280:MAIN_COLS])
    ab_ref[...] = _dot(h, wab_ref[...])


def _in_proj(x_all, mod_l, w_main, w_ab):
    tm = TOKEN_TILE
    return pl.pallas_call(
        _inproj_kernel,
        grid=(N_TILES,),
        in_specs=[
            pl.BlockSpec((tm, D_MODEL), lambda i: (i, 0)),
            pl.BlockSpec((None, 1, 6 * D_MODEL), lambda i: (_mod_row(i), 0, 0)),
            pl.BlockSpec((D_MODEL, MAIN_COLS), lambda i: (0, 0)),
            pl.BlockSpec((D_MODEL, LANES), lambda i: (0, 0)),
        ],
        out_specs=[
            pl.BlockSpec((tm, 1024), lambda i: (i, 0)),
            pl.BlockSpec((tm, 256), lambda i: (i, 0)),
            pl.BlockSpec((tm, 2048), lambda i: (i, 0)),
            pl.BlockSpec((tm, LANES), lambda i: (i, 0)),
        ],
        out_shape=[
            jax.ShapeDtypeStruct((T_ALL, 1024), F32),
            jax.ShapeDtypeStruct((T_ALL, 256), F32),
            jax.ShapeDtypeStruct((T_ALL, 2048), F32),
            jax.ShapeDtypeStruct((T_ALL, LANES), F32),
        ],
        compiler_params=_params("arbitrary"),
        name="in_proj",
    )(x_all, mod_l, w_main, w_ab)


RET_TQ = 256


def _ret_kernel(*refs, L, rope, has_init, emit_state):
    it = iter(refs)
    x_ref, dec_ref = next(it), next(it)
    cos_ref = next(it) if rope else None
    sin_ref = next(it) if rope else None
    r0_ref = next(it) if has_init else None
    y_ref = next(it)
    st_ref = next(it) if emit_state else None

    lg = -jnp.exp(dec_ref[...])
    q = x_ref[:, 0:256]
    k = x_ref[:, 256:512] * (RET_DK ** -0.5)
    v = x_ref[:, 512:768]
    if rope:
        lane = lax.broadcasted_iota(jnp.int32, (L, RET_WIDTH), 1)
        first_half = (lane % RET_DK) < (RET_DK // 2)
        cos, sin = cos_ref[...], sin_ref[...]

        def rot(z):
            swapped = jnp.where(first_half, pltpu.roll(z, RET_WIDTH - RET_DK // 2, axis=1),
                                pltpu.roll(z, RET_DK // 2, axis=1))
            return z * cos + swapped * sin

        q, k = rot(q), rot(k)
    kb16 = k.astype(BF16)
    vb16 = v.astype(BF16)
    tcol_all = lax.broadcasted_iota(jnp.int32, (L, 1), 0).astype(F32)

    for qb in range(L // RET_TQ):
        rows = slice(qb * RET_TQ, (qb + 1) * RET_TQ)
        t_i = lax.broadcasted_iota(jnp.int32, (RET_TQ, L), 0) + qb * RET_TQ
        s_i = lax.broadcasted_iota(jnp.int32, (RET_TQ, L), 1)
        dist = (t_i - s_i).astype(F32)
        diag = jnp.where(t_i == s_i, 1.0, 0.0)
        tcol = tcol_all[rows]
        pieces = []
        for h in range(RET_HEADS):
            cols = slice(h * RET_DK, (h + 1) * RET_DK)
            lgf = lg[0:1, h:h + 1]
            lgb = lg[1:2, h:h + 1]
            qh = q[rows, cols]
            scores = _dot_nt(qh.astype(BF16), kb16[:, cols])
            decay = jnp.exp(jnp.where(dist >= 0, lgf * dist, -lgb * dist)) + diag
            o = _dot((scores * decay).astype(BF16), vb16[:, cols])
            if has_init:
                qf = (qh * jnp.exp(lgf * (tcol + 1.0))).astype(BF16)
                qr = (qh * jnp.exp(lgb * (float(L) - tcol))).astype(BF16)
                o = o + _dot(qf, r0_ref[0, h].astype(BF16)) + _dot(qr, r0_ref[1, h].astype(BF16))
            pieces.append(_ln(o) * _silu(x_ref[rows, 768 + h * RET_DK:768 + (h + 1) * RET_DK]))
        y_ref[rows, :] = jnp.concatenate(pieces, axis=1).astype(BF16)

    if emit_state:
        for h in range(RET_HEADS):
            cols = slice(h * RET_DK, (h + 1) * RET_DK)
            wf = jnp.exp(lg[0:1, h:h + 1] * (float(L - 1) - tcol_all))
            wb = jnp.exp(lg[1:2, h:h + 1] * tcol_all)
            st_ref[0, h] = _dot_tn((k[:, cols] * wf).astype(BF16), vb16[:, cols])
            st_ref[1, h] = _dot_tn((k[:, cols] * wb).astype(BF16), vb16[:, cols])


def _retention(ret_in, ret_decay_l, layer, *, latent, rope_tabs=None, state=None):
    L = L_LAT if latent else L_CTX
    nb = N_LAT if latent else N_CTX
    off = T_CTX // L if latent else 0
    kern = functools.partial(_ret_kernel, L=L, rope=latent, has_init=latent, emit_state=not latent)
    in_specs = [pl.BlockSpec((L, 1024), lambda b: (b + off, 0)),
                pl.BlockSpec((2, RET_HEADS), lambda b: (0, 0))]
    args = [ret_in, ret_decay_l]
    if latent:
        in_specs += [pl.BlockSpec((L, RET_WIDTH), lambda b: (0, 0))] * 2
        args += list(rope_tabs)
        in_specs.append(pl.BlockSpec((None, None, 2, RET_HEADS, RET_DK, RET_DK),
                                     lambda b: (b, layer, 0, 0, 0, 0)))
        args.append(state)
    out_specs = [pl.BlockSpec((L, RET_WIDTH), lambda b: (b, 0))]
    out_shape = [jax.ShapeDtypeStruct((nb * L, RET_WIDTH), BF16)]
    if not latent:
        out_specs.append(pl.BlockSpec((None, 2, RET_HEADS, RET_DK, RET_DK), lambda b: (b, 0, 0, 0, 0)))
        out_shape.append(jax.ShapeDtypeStruct((nb, 2, RET_HEADS, RET_DK, RET_DK), F32))
    return pl.pallas_call(
        kern, grid=(nb,), in_specs=in_specs, out_specs=out_specs, out_shape=out_shape,
        compiler_params=_params("arbitrary"),
        name="retention_lat" if latent else "retention_ctx",
    )(*args)


def _s5_prep_kernel(are_ref, aim_ref, ldt_ref, bre_ref, bim_ref, abar_ref, wb_ref):
    are, aim = are_ref[...], aim_ref[...]
    dt = jnp.exp(ldt_ref[...])
    mag = jnp.exp(are * dt)
    abr = mag * jnp.cos(aim * dt)
    abi = mag * jnp.sin(aim * dt)
    den = are * are + aim * aim
    cr = ((abr - 1.0) * are + abi * aim) / den
    ci = (abi * are - (abr - 1.0) * aim) / den
    bre, bim = bre_ref[...], bim_ref[...]
    for d in range(2):
        crd, cid = cr[d:d + 1], ci[d:d + 1]
        wb_ref[d, :, 0:S5_STATE] = (crd * bre - cid * bim).astype(BF16)
        wb_ref[d, :, S5_STATE:2 * S5_STATE] = (crd * bim + cid * bre).astype(BF16)
        abar_ref[2 * d:2 * d + 1, :] = abr[d:d + 1]
        abar_ref[2 * d + 1:2 * d + 2, :] = abi[d:d + 1]


def _s5_prep(a_re, a_im, ldt_rep, bre_bd, bim_bd):
    vec = pl.BlockSpec((None, 2, S5_STATE), lambda l: (l, 0, 0))
    mat = pl.BlockSpec((None, S5_WIDTH, S5_STATE), lambda l: (l, 0, 0))
    return pl.pallas_call(
        _s5_prep_kernel,
        grid=(DEPTH,),
        in_specs=[vec, vec, vec, mat, mat],
        out_specs=[pl.BlockSpec((None, 4, S5_STATE), lambda l: (l, 0, 0)),
                   pl.BlockSpec((None, 2, S5_WIDTH, 2 * S5_STATE), lambda l: (l, 0, 0, 0))],
        out_shape=[jax.ShapeDtypeStruct((DEPTH, 4, S5_STATE), F32),
                   jax.ShapeDtypeStruct((DEPTH, 2, S5_WIDTH, 2 * S5_STATE), BF16)],
        compiler_params=_params("arbitrary"),
        name="s5_prep",
    )(a_re, a_im, ldt_rep, bre_bd, bim_bd)


def _gelu_tanh(x):
    return x * (0.5 * (1.0 + jnp.tanh(np.sqrt(2.0 / np.pi).astype(np.float32) * (x + 0.044715 * (x * x * x)))))


def _s5_kernel(*refs, L, has_init, emit_state):
    it = iter(refs)
    u_ref, abar_ref, wb_ref, wcre_ref, wcim_ref, d_ref, wglu_ref = (next(it) for _ in range(7))
    x0re_ref = next(it) if has_init else None
    x0im_ref = next(it) if has_init else None
    y_ref = next(it)
    fre_ref = next(it) if emit_state else None
    fim_ref = next(it) if emit_state else None
    sf_ref, sb_ref = next(it), next(it)

    u = u_ref[...]
    ub = u.astype(BF16)
    sf_ref[...] = _dot(ub, wb_ref[0])
    sb_ref[...] = _dot(ub, wb_ref[1])
    afr, afi = abar_ref[0:1, :], abar_ref[1:2, :]
    abr, abi = abar_ref[2:3, :], abar_ref[3:4, :]
    if has_init:
        init = (x0re_ref[0:1, :], x0im_ref[0:1, :], x0re_ref[1:2, :], x0im_ref[1:2, :])
    else:
        init = tuple(jnp.zeros((1, S5_STATE), F32) for _ in range(4))

    def body(t, carry):
        xfr, xfi, xbr, xbi = carry
        tb = L - 1 - t
        nfr = afr * xfr - afi * xfi + sf_ref[pl.ds(t, 1), 0:S5_STATE]
        nfi = afr * xfi + afi * xfr + sf_ref[pl.ds(t, 1), S5_STATE:2 * S5_STATE]
        nbr = abr * xbr - abi * xbi + sb_ref[pl.ds(tb, 1), 0:S5_STATE]
        nbi = abr * xbi + abi * xbr + sb_ref[pl.ds(tb, 1), S5_STATE:2 * S5_STATE]
        sf_ref[pl.ds(t, 1), 0:S5_STATE] = nfr
        sf_ref[pl.ds(t, 1), S5_STATE:2 * S5_STATE] = nfi
        sb_ref[pl.ds(tb, 1), 0:S5_STATE] = nbr
        sb_ref[pl.ds(tb, 1), S5_STATE:2 * S5_STATE] = nbi
        return nfr, nfi, nbr, nbi

    xfr, xfi, xbr, xbi = lax.fori_loop(0, L, body, init, unroll=4)
    if emit_state:
        fre_ref[0:1, :] = xfr
        fre_ref[1:2, :] = xbr
        fim_ref[0:1, :] = xfi
        fim_ref[1:2, :] = xbi

    states = (sf_ref[...] + sb_ref[...]).astype(BF16)
    y = _dot(states[:, 0:S5_STATE], wcre_ref[...]) - _dot(states[:, S5_STATE:2 * S5_STATE], wcim_ref[...])
    y = _gelu_tanh(y + d_ref[...] * u)
    y_ref[...] = (y * jax.nn.sigmoid(_dot(y.astype(BF16), wglu_ref[...]))).astype(BF16)


def _s5(s5_in, abar_l, wb_l, wcre_l, wcim_l, d_l, wglu_l, layer, *, latent, x0re=None, x0im=None):
    L = L_LAT if latent else L_CTX
    nb = N_LAT if latent else N_CTX
    off = T_CTX // L if latent else 0
    kern = functools.partial(_s5_kernel, L=L, has_init=latent, emit_state=not latent)
    const2 = lambda b: (0, 0)
    in_specs = [pl.BlockSpec((L, S5_WIDTH), lambda b: (b + off, 0)),
                pl.BlockSpec((4, S5_STATE), const2),
                pl.BlockSpec((2, S5_WIDTH, 2 * S5_STATE), lambda b: (0, 0, 0)),
                pl.BlockSpec((S5_STATE, S5_WIDTH), const2),
                pl.BlockSpec((S5_STATE, S5_WIDTH), const2),
                pl.BlockSpec((1, S5_WIDTH), const2),
                pl.BlockSpec((S5_WIDTH, S5_WIDTH), const2)]
    args = [s5_in, abar_l, wb_l, wcre_l, wcim_l, d_l, wglu_l]
    if latent:
        st = pl.BlockSpec((None, None, 2, S5_STATE), lambda b: (b, layer, 0, 0))
        in_specs += [st, st]
        args += [x0re, x0im]
    out_specs = [pl.BlockSpec((L, S5_WIDTH), lambda b: (b, 0))]
    out_shape = [jax.ShapeDtypeStruct((nb * L, S5_WIDTH), BF16)]
    if not latent:
        fs = pl.BlockSpec((None, 2, S5_STATE), lambda b: (b, 0, 0))
        out_specs += [fs, fs]
        out_shape += [jax.ShapeDtypeStruct((nb, 2, S5_STATE), F32)] * 2
    return pl.pallas_call(
        kern, grid=(nb,), in_specs=in_specs, out_specs=out_specs, out_shape=out_shape,
        scratch_shapes=[pltpu.VMEM((L, 2 * S5_STATE), F32), pltpu.VMEM((L, 2 * S5_STATE), F32)],
        compiler_params=_params("arbitrary"),
        name="s5_lat" if latent else "s5_ctx",
    )(*args)


def _softplus(x):
    return jnp.maximum(x, 0.0) + jnp.log1p(jnp.exp(-jnp.abs(x)))


def _unit_tri_inverse(m):
    row = lax.broadcasted_iota(jnp.int32, (CHUNK, CHUNK), 0)
    col = lax.broadcasted_iota(jnp.int32, (CHUNK, CHUNK), 1)
    p = -m
    t = jnp.where(row == col, 1.0, 0.0) + p
    for _ in range(5):
        p = _dot_f32(p, p)
        t = t + _dot_f32(p, t)
    return t


def _gdn_kernel(*refs, L, has_init, emit_state):
    it = iter(refs)
    x_ref, ab_ref, conv_ref, alog_ref, dtb_ref, norm_ref = (next(it) for _ in range(6))
    s0_ref = next(it) if has_init else None
    y_ref = next(it)
    sfin_ref = next(it) if emit_state else None
    q_s, k_s, v_s, gf_s, gb_s, beta_s, of_s, ob_s, st_s = (next(it) for _ in range(9))
    n = L // CHUNK
    qkv = 3 * GDN_WIDTH

    w0, w1, w2 = conv_ref[0:1, :], conv_ref[1:2, :], conv_ref[2:3, :]
    rate_f, rate_b = jnp.exp(alog_ref[0:1, :]), jnp.exp(alog_ref[1:2, :])
    dtb_f, dtb_b = dtb_ref[0:1, :], dtb_ref[1:2, :]
    row_q = lax.broadcasted_iota(jnp.int32, (CHUNK, qkv), 0)

    def prep(c, _):
        r0 = pl.multiple_of(c * CHUNK, CHUNK)
        xc = x_ref[pl.ds(r0, CHUNK), 0:qkv]
        prev = x_ref[pl.ds(jnp.maximum(r0 - 1, 0), 1), 0:qkv] * (c > 0).astype(F32)
        nxt = x_ref[pl.ds(jnp.minimum(r0 + CHUNK, L - 1), 1), 0:qkv] * (c < n - 1).astype(F32)
        xm1 = jnp.where(row_q == 0, prev, pltpu.roll(xc, 1, axis=0))
        xp1 = jnp.where(row_q == CHUNK - 1, nxt, pltpu.roll(xc, CHUNK - 1, axis=0))
        y = _silu(w0 * xm1 + w1 * xc + w2 * xp1)
        qs, ks = [], []
        for h in range(GDN_HEADS):
            qh = y[:, h * GDN_DK:(h + 1) * GDN_DK]
            kh = y[:, GDN_WIDTH + h * GDN_DK:GDN_WIDTH + (h + 1) * GDN_DK]
            qs.append(qh * lax.rsqrt(jnp.sum(qh * qh, -1, keepdims=True) + 1e-6) * (GDN_DK ** -0.5))
            ks.append(kh * lax.rsqrt(jnp.sum(kh * kh, -1, keepdims=True) + 1e-6))
        q_s[pl.ds(r0, CHUNK), :] = jnp.concatenate(qs, axis=1)
        k_s[pl.ds(r0, CHUNK), :] = jnp.concatenate(ks, axis=1)
        v_s[pl.ds(r0, CHUNK), :] = y[:, 2 * GDN_WIDTH:qkv]
        ab = ab_ref[pl.ds(r0, CHUNK), :]
        gf_s[pl.ds(r0, CHUNK), :] = -rate_f * _softplus(ab + dtb_f)
        gb_s[pl.ds(r0, CHUNK), :] = -rate_b * _softplus(ab + dtb_b)
        beta_s[pl.ds(r0, CHUNK), :] = jax.nn.sigmoid(ab)
        return 0

    lax.fori_loop(0, n, prep, 0)

    if has_init:
        st_s[...] = s0_ref[...]
    else:
        st_s[...] = jnp.zeros(st_s.shape, F32)
    row = lax.broadcasted_iota(jnp.int32, (CHUNK, CHUNK), 0)
    col = lax.broadcasted_iota(jnp.int32, (CHUNK, CHUNK), 1)
    zpad = jnp.zeros((LANES - CHUNK, LANES), F32)

    def step(i, _):
        for d in range(2):
            c = i if d == 0 else n - 1 - i
            r0 = pl.multiple_of(c * CHUNK, CHUNK)
            incl = (row >= col) if d == 0 else (row <= col)
            strict = (row > col) if d == 0 else (row < col)
            g_s, o_s = (gf_s, of_s) if d == 0 else (gb_s, ob_s)
            last = CHUNK - 1 if d == 0 else 0
            gc = _dot_f32(jnp.where(incl, 1.0, 0.0), g_s[pl.ds(r0, CHUNK), :])
            gct = jnp.concatenate([gc, zpad], axis=0).T
            beta = beta_s[pl.ds(r0, CHUNK), :]
            outs = []
            for h in range(GDN_HEADS):
                cols = slice(h * GDN_DK, (h + 1) * GDN_DK)
                qh = q_s[pl.ds(r0, CHUNK), cols]
                kh = k_s[pl.ds(r0, CHUNK), cols]
                vh = v_s[pl.ds(r0, CHUNK), cols]
                bh = beta[:, GDN_HEADS + h:GDN_HEADS + h + 1]
                gcol = gc[:, h:h + 1]
                grow = gct[h:h + 1, 0:CHUNK]
                dmat = jnp.where(incl, jnp.exp(jnp.where(incl, gcol - grow, 0.0)), 0.0)
                eg = jnp.exp(gcol)
                kbeta = kh * bh
                k16 = kh.astype(BF16)
                m = jnp.where(strict, _dot_nt(kbeta.astype(BF16), k16) * dmat, 0.0)
                attn = jnp.where(incl, _dot_nt(qh.astype(BF16), k16) * dmat, 0.0)
                tinv = _unit_tri_inverse(m)
                w = _dot_f32(tinv, kbeta * eg)
                u = _dot_f32(tinv, vh * bh)
                s = st_s[d, h]
                s16 = s.astype(BF16)
                v_new = u - _dot(w.astype(BF16), s16)
                vn16 = v_new.astype(BF16)
                outs.append(_dot((qh * eg).astype(BF16), s16) + _dot(attn.astype(BF16), vn16))
                glast = gc[last:last + 1, h:h + 1]
                k_dec = kh * jnp.exp(glast - gcol)
                st_s[d, h] = s * jnp.exp(glast) + _dot_tn(k_dec.astype(BF16), vn16)
            o_s[pl.ds(r0, CHUNK), :] = jnp.concatenate(outs, axis=1)
        return 0

    lax.fori_loop(0, n, step, 0)
    if emit_state:
        sfin_ref[...] = st_s[...]

    gain = norm_ref[...]

    def fin(c, _):
        r0 = pl.multiple_of(c * CHUNK, CHUNK)
        o = of_s[pl.ds(r0, CHUNK), :] + ob_s[pl.ds(r0, CHUNK), :]
        z = x_ref[pl.ds(r0, CHUNK), qkv:qkv + GDN_WIDTH]
        pieces = []
        for h in range(GDN_HEADS):
            cols = slice(h * GDN_DK, (h + 1) * GDN_DK)
            oh = o[:, cols]
            on = oh * lax.rsqrt(jnp.mean(oh * oh, -1, keepdims=True) + LN_EPS) * gain
            pieces.append(on * _silu(z[:, cols]))
        y_ref[pl.ds(r0, CHUNK), :] = jnp.concatenate(pieces, axis=1).astype(BF16)
        return 0

    lax.fori_loop(0, n, fin, 0)


def _gdn(gdn_in, ab_in, conv_l, alog_l, dtb_l, norm_l, layer, *, latent, state=None):
    L = L_LAT if latent else L_CTX
    nb = N_LAT if latent else N_CTX
    off = T_CTX // L if latent else 0
    kern = functools.partial(_gdn_kernel, L=L, has_init=latent, emit_state=not latent)
    const2 = lambda b: (0, 0)
    in_specs = [pl.BlockSpec((L, 2048), lambda b: (b + off, 0)),
                pl.BlockSpec((L, LANES), lambda b: (b + off, 0)),
                pl.BlockSpec((3, 3 * GDN_WIDTH), const2),
                pl.BlockSpec((2, LANES), const2),
                pl.BlockSpec((2, LANES), const2),
                pl.BlockSpec((1, GDN_DK), const2)]
    args = [gdn_in, ab_in, conv_l, alog_l, dtb_l, norm_l]
    if latent:
        in_specs.append(pl.BlockSpec((None, None, 2, GDN_HEADS, GDN_DK, GDN_DK),
                                     lambda b: (b, layer, 0, 0, 0, 0)))
        args.append(state)
    out_specs = [pl.BlockSpec((L, GDN_WIDTH), lambda b: (b, 0))]
    out_shape = [jax.ShapeDtypeStruct((nb * L, GDN_WIDTH), BF16)]
    if not latent:
        out_specs.append(pl.BlockSpec((None, 2, GDN_HEADS, GDN_DK, GDN_DK), lambda b: (b, 0, 0, 0, 0)))
        out_shape.append(jax.ShapeDtypeStruct((nb, 2, GDN_HEADS, GDN_DK, GDN_DK), F32))
    wide = pltpu.VMEM((L, GDN_WIDTH), F32)
    narrow = pltpu.VMEM((L, LANES), F32)
    return pl.pallas_call(
        kern, grid=(nb,), in_specs=in_specs, out_specs=out_specs, out_shape=out_shape,
        scratch_shapes=[wide, wide, wide, narrow, narrow, narrow, wide, wide,
                        pltpu.VMEM((2, GDN_HEADS, GDN_DK, GDN_DK), F32)],
        compiler_params=_params("arbitrary"),
        name="gdn_lat" if latent else "gdn_ctx",
    )(*args)


def _post_kernel(x_ref, yr_ref, ys_ref, yg_ref, mod_ref, wout_ref, g_ref, b_ref, wr_ref, br_ref,
                 x1_ref, h2_ref, gate_ref):
    mix = (_dot(yr_ref[...], wout_ref[0:256, :]) + _dot(ys_ref[...], wout_ref[256:512, :])
           + _dot(yg_ref[...], wout_ref[512:1024, :]))
    mod = mod_ref[...]
    g1 = mod[:, 2 * D_MODEL:3 * D_MODEL]
    sh2 = mod[:, 3 * D_MODEL:4 * D_MODEL]
    sc2 = mod[:, 4 * D_MODEL:5 * D_MODEL]
    x1 = _ln(ALPHA * x_ref[...] + g1 * mix) * g_ref[...] + b_ref[...]
    x1_ref[...] = x1
    h2 = _ln(x1) * (1.0 + sc2) + sh2
    h2_ref[...] = h2.astype(BF16)

    logits = _dot_f32(h2, wr_ref[...]) + br_ref[...]
    lane = lax.broadcasted_iota(jnp.int32, logits.shape, 1)
    lane_f = lane.astype(F32)
    big = float(LANES)
    neg = -jnp.inf
    is_g = (lane >= N_EXPERTS) & (lane < N_EXPERTS + N_EGROUPS)
    gl = jnp.where(is_g, logits, neg)
    gmax = jnp.max(gl, -1, keepdims=True)
    g_sel = jnp.min(jnp.where(is_g & (gl == gmax), lane_f, big), -1, keepdims=True) - float(N_EXPERTS)
    p_group = 1.0 / jnp.sum(jnp.where(is_g, jnp.exp(gl - gmax), 0.0), -1, keepdims=True)
    in_sel = (lane < N_EXPERTS) & ((lane >> 2).astype(F32) == g_sel)
    el = jnp.where(in_sel, logits, neg)
    emax = jnp.max(el, -1, keepdims=True)
    ee = jnp.where(in_sel, jnp.exp(el - emax), 0.0)
    prob = ee / jnp.sum(ee, -1, keepdims=True)
    p1 = jnp.max(jnp.where(in_sel, prob, -1.0), -1, keepdims=True)
    i1 = jnp.min(jnp.where(in_sel & (prob == p1), lane_f, big), -1, keepdims=True)
    rest = in_sel & (lane_f != i1)
    p2 = jnp.max(jnp.where(rest, prob, -1.0), -1, keepdims=True)
    i2 = jnp.min(jnp.where(rest & (prob == p2), lane_f, big), -1, keepdims=True)
    den = p1 + p2
    gate_ref[...] = jnp.where(lane_f == i1, p_group * p1 / den,
                              jnp.where(lane_f == i2, p_group * p2 / den, 0.0))


def _post(x_all, y_ret, y_s5, y_gdn, mod_l, w_out_l, ln_g, ln_b, wr_l, br_l):
    tm = TOKEN_TILE
    const2 = lambda i: (0, 0)
    return pl.pallas_call(
        _post_kernel,
        grid=(N_TILES,),
        in_specs=[
            pl.BlockSpec((tm, D_MODEL), lambda i: (i, 0)),
            pl.BlockSpec((tm, RET_WIDTH), lambda i: (i, 0)),
            pl.BlockSpec((tm, S5_WIDTH), lambda i: (i, 0)),
            pl.BlockSpec((tm, GDN_WIDTH), lambda i: (i, 0)),
            pl.BlockSpec((None, 1, 6 * D_MODEL), lambda i: (_mod_row(i), 0, 0)),
            pl.BlockSpec((D_MODEL, D_MODEL), const2),
            pl.BlockSpec((1, D_MODEL), const2),
            pl.BlockSpec((1, D_MODEL), const2),
            pl.BlockSpec((D_MODEL, LANES), const2),
            pl.BlockSpec((1, LANES), const2),
        ],
        out_specs=[
            pl.BlockSpec((tm, D_MODEL), lambda i: (i, 0)),
            pl.BlockSpec((tm, D_MODEL), lambda i: (i, 0)),
            pl.BlockSpec((tm, LANES), lambda i: (i, 0)),
        ],
        out_shape=[
            jax.ShapeDtypeStruct((T_ALL, D_MODEL), F32),
            jax.ShapeDtypeStruct((T_ALL, D_MODEL), BF16),
            jax.ShapeDtypeStruct((T_ALL, LANES), F32),
        ],
        compiler_params=_params("arbitrary"),
        name="post_mix",
    )(x_all, y_ret, y_s5, y_gdn, mod_l, w_out_l, ln_g, ln_b, wr_l, br_l)


def _moe_kernel(h_ref, gate_ref, x1_ref, mod_ref, wg_ref, wu_ref, wd_ref, g_ref, b_ref, o_ref, acc_ref):
    e = pl.program_id(1)

    @pl.when(e == 0)
    def _():
        acc_ref[...] = jnp.zeros(acc_ref.shape, F32)

    h = h_ref[...]
    gate = gate_ref[...]
    lane = lax.broadcasted_iota(jnp.int32, gate.shape, 1)
    ge = jnp.sum(jnp.where(lane == e, gate, 0.0), -1, keepdims=True)
    act = _silu(_dot(h, wg_ref[...])) * _dot(h, wu_ref[...]) * ge
    acc_ref[...] += _dot(act.astype(BF16), wd_ref[...])

    @pl.when(e == N_EXPERTS - 1)
    def _():
        g2 = mod_ref[:, 5 * D_MODEL:6 * D_MODEL]
        o_ref[...] = _ln(ALPHA * x1_ref[...] + g2 * acc_ref[...]) * g_ref[...] + b_ref[...]


def _moe(h2, gate, x1, mod_l, wg_l, wu_l, wd_l, ln_g, ln_b):
    tm = TOKEN_TILE
    const2 = lambda i, e: (0, 0)
    return pl.pallas_call(
        _moe_kernel,
        grid=(N_TILES, N_EXPERTS),
        in_specs=[
            pl.BlockSpec((tm, D_MODEL), lambda i, e: (i, 0)),
            pl.BlockSpec((tm, LANES), lambda i, e: (i, 0)),
            pl.BlockSpec((tm, D_MODEL), lambda i, e: (i, 0)),
            pl.BlockSpec((None, 1, 6 * D_MODEL), lambda i, e: (_mod_row(i), 0, 0)),
            pl.BlockSpec((None, D_MODEL, D_FF), lambda i, e: (e, 0, 0)),
            pl.BlockSpec((None, D_MODEL, D_FF), lambda i, e: (e, 0, 0)),
            pl.BlockSpec((None, D_FF, D_MODEL), lambda i, e: (e, 0, 0)),
            pl.BlockSpec((1, D_MODEL), const2),
            pl.BlockSpec((1, D_MODEL), const2),
        ],
        out_specs=pl.BlockSpec((tm, D_MODEL), lambda i, e: (i, 0)),
        out_shape=jax.ShapeDtypeStruct((T_ALL, D_MODEL), F32),
        scratch_shapes=[pltpu.VMEM((tm, D_MODEL), F32)],
        compiler_params=_params("arbitrary", "arbitrary"),
        name="moe",
    )(h2, gate, x1, mod_l, wg_l, wu_l, wd_l, ln_g, ln_b)


def _rope_tables():
    rows = L_LAT // GRID_W
    row = jnp.repeat(jnp.arange(rows, dtype=F32), GRID_W)
    col = (jnp.arange(rows * GRID_W) % GRID_W).astype(F32)
    n_freq = RET_DK // 4
    inv_freq = ROPE_BASE ** (-jnp.arange(n_freq, dtype=F32) / n_freq)
    ang = jnp.concatenate([row[:, None] * inv_freq, col[:, None] * inv_freq], -1)
    cos, sin = jnp.cos(ang), jnp.sin(ang)
    cos_t = jnp.tile(jnp.concatenate([cos, cos], -1), (1, RET_HEADS))
    sin_t = jnp.tile(jnp.concatenate([-sin, sin], -1), (1, RET_HEADS))
    return cos_t, sin_t


def _block_diag_in(b):
    eye = jnp.eye(S5_GROUPS, dtype=F32)
    bt = jnp.transpose(b, (0, 1, 3, 2))
    return (bt[:, :, :, None, :] * eye[None, :, None, :, None]).reshape(DEPTH, S5_WIDTH, S5_STATE)


def _block_diag_out(c):
    eye = jnp.eye(S5_GROUPS, dtype=F32)
    ct = jnp.transpose(c, (0, 1, 3, 2))
    return (ct[:, :, :, None, :] * eye[None, :, None, :, None]).reshape(DEPTH, S5_STATE, S5_WIDTH)


def _pad_lanes(a):
    return jnp.pad(a, [(0, 0)] * (a.ndim - 1) + [(0, LANES - a.shape[-1])])


def kernel(x_prompt, x_sample, state_ret, state_s5_re, state_s5_im, state_gdn, c, c_ctx, w_ada, b_ada, w_in, ret_decay, s5_a_re, s5_a_im, s5_log_dt, s5_b_re, s5_b_im, s5_c_re, s5_c_im, s5_d, s5_w_glu, gdn_conv, gdn_a_log, gdn_dt_bias, gdn_norm, w_out, ln1_g, ln1_b, ln2_g, ln2_b, router_group_w, router_group_b, router_expert_w, router_expert_b, w_gate, w_up, w_down):
    x_all = jnp.concatenate([x_prompt.reshape(T_CTX, D_MODEL), x_sample.reshape(T_LAT, D_MODEL)], 0)
    cond8 = jnp.concatenate([c, c_ctx[None, :], jnp.zeros((8 - N_LAT - 1, D_MODEL), F32)], 0)
    mod = _ada_table(cond8, w_ada, b_ada).reshape(DEPTH, 8, 1, 6 * D_MODEL)

    w_main = w_in[:, :, :MAIN_COLS].astype(BF16)
    w_ab = _pad_lanes(w_in[:, :, MAIN_COLS:]).astype(BF16)
    w_out16 = w_out.astype(BF16)
    wg16, wu16, wd16 = w_gate.astype(BF16), w_up.astype(BF16), w_down.astype(BF16)
    wr = _pad_lanes(jnp.concatenate([router_expert_w, router_group_w], -1))
    br = _pad_lanes(jnp.concatenate([router_expert_b, router_group_b], -1)).reshape(DEPTH, 1, LANES)
    rope_tabs = _rope_tables()

    abar, wb = _s5_prep(s5_a_re.reshape(DEPTH, 2, S5_STATE), s5_a_im.reshape(DEPTH, 2, S5_STATE),
                        jnp.repeat(s5_log_dt, S5_P, axis=-1),
                        _block_diag_in(s5_b_re), _block_diag_in(s5_b_im))
    wcre = _block_diag_out(s5_c_re).astype(BF16)
    wcim = _block_diag_out(s5_c_im).astype(BF16)
    wglu16 = s5_w_glu.astype(BF16)
    x0re = state_s5_re.reshape(N_LAT, DEPTH, 2, S5_STATE)
    x0im = state_s5_im.reshape(N_LAT, DEPTH, 2, S5_STATE)
    alog = _pad_lanes(gdn_a_log)
    dtb = _pad_lanes(gdn_dt_bias)

    ret_l, s5re_l, s5im_l, gdn_l = [], [], [], []
    for l in range(DEPTH):
        ret_in, s5_in, gdn_in, ab_in = _in_proj(x_all, mod[l], w_main[l], w_ab[l])

        yr_c, st_ret = _retention(ret_in, ret_decay[l], l, latent=False)
        (yr_l,) = _retention(ret_in, ret_decay[l], l, latent=True, rope_tabs=rope_tabs, state=state_ret)
        s5_args = (abar[l], wb[l], wcre[l], wcim[l], s5_d[l].reshape(1, S5_WIDTH), wglu16[l], l)
        ys_c, fre, fim = _s5(s5_in, *s5_args, latent=False)
        (ys_l,) = _s5(s5_in, *s5_args, latent=True, x0re=x0re, x0im=x0im)
        gdn_args = (gdn_conv[l], alog[l], dtb[l], gdn_norm[l].reshape(1, GDN_DK), l)
        yg_c, st_gdn = _gdn(gdn_in, ab_in, *gdn_args, latent=False)
        (yg_l,) = _gdn(gdn_in, ab_in, *gdn_args, latent=True, state=state_gdn)

        x1, h2, gate = _post(x_all, jnp.concatenate([yr_c, yr_l], 0), jnp.concatenate([ys_c, ys_l], 0),
                             jnp.concatenate([yg_c, yg_l], 0), mod[l], w_out16[l],
                             ln1_g[l].reshape(1, D_MODEL), ln1_b[l].reshape(1, D_MODEL), wr[l], br[l])
        x_all = _moe(h2, gate, x1, mod[l], wg16[l], wu16[l], wd16[l],
                     ln2_g[l].reshape(1, D_MODEL), ln2_b[l].reshape(1, D_MODEL))

        ret_l.append(st_ret)
        s5re_l.append(fre.reshape(N_CTX, 2, S5_GROUPS, S5_P))
        s5im_l.append(fim.reshape(N_CTX, 2, S5_GROUPS, S5_P))
        gdn_l.append(st_gdn)

    y_prompt = x_all[:T_CTX].reshape(N_CTX, L_CTX, D_MODEL)
    y_sample = x_all[T_CTX:].reshape(N_LAT, L_LAT, D_MODEL)
    return (y_prompt, y_sample, jnp.stack(ret_l, 1), jnp.stack(s5re_l, 1), jnp.stack(s5im_l, 1),
            jnp.stack(gdn_l, 1))
```

```python
import functools

import jax
import jax.numpy as jnp
import numpy as np
from jax import lax
from jax.experimental import pallas as pl
from jax.experimental.pallas import tpu as pltpu

F32 = jnp.float32
BF16 = jnp.bfloat16
HIGHEST = lax.Precision.HIGHEST

D_MODEL = 1024
DEPTH = 4
N_CTX, L_CTX = 32, 256
N_LAT, L_LAT = 4, 1024
T_CTX = N_CTX * L_CTX
T_LAT = N_LAT * L_LAT
T_ALL = T_CTX + T_LAT
GRID_W = 64
CHUNK = 64
ROPE_BASE = 10000.0
ALPHA = (2 * DEPTH) ** 0.25
LN_EPS = 1e-6
RET_HEADS, RET_DK = 4, 64
RET_WIDTH = 256
S5_WIDTH, S5_CH, S5_GROUPS, S5_P = 256, 16, 16, 64
S5_STATE = S5_GROUPS * S5_P
GDN_HEADS, GDN_DK = 8, 64
GDN_WIDTH = 512
N_EXPERTS, EXPERTS_PER_GROUP, N_EGROUPS = 16, 4, 4
D_FF = 512
MAIN_COLS = 3328
LANES = 128
TOKEN_TILE = 512
N_TILES = T_ALL // TOKEN_TILE
CTX_TILES = T_CTX // TOKEN_TILE
TILES_PER_LAT = L_LAT // TOKEN_TILE
CTX_MOD_ROW = N_LAT
VMEM_LIMIT = 56 * 1024 * 1024


def _params(*sem):
    return pltpu.CompilerParams(dimension_semantics=sem, vmem_limit_bytes=VMEM_LIMIT)


def _ln(x):
    mu = jnp.mean(x, -1, keepdims=True)
    xc = x - mu
    var = jnp.mean(xc * xc, -1, keepdims=True)
    return xc * lax.rsqrt(var + LN_EPS)


def _silu(x):
    return x * jax.nn.sigmoid(x)


def _dot(a, b):
    return jnp.dot(a, b, preferred_element_type=F32)


def _dot_f32(a, b):
    return jnp.dot(a, b, precision=HIGHEST, preferred_element_type=F32)


def _dot_nt(a, b):
    return lax.dot_general(a, b, (((1,), (1,)), ((), ())), preferred_element_type=F32)


def _dot_tn(a, b):
    return lax.dot_general(a, b, (((0,), (0,)), ((), ())), preferred_element_type=F32)


def _mod_row(i):
    return jnp.where(i < CTX_TILES, CTX_MOD_ROW, (i - CTX_TILES) // TILES_PER_LAT)


ADA_TN = 1536


def _ada_kernel(c_ref, w_ref, b_ref, o_ref):
    o_ref[...] = _dot_f32(_silu(c_ref[...]), w_ref[...]) + b_ref[...]


def _ada_table(cond8, w_ada, b_ada):
    return pl.pallas_call(
        _ada_kernel,
        grid=(DEPTH, 6 * D_MODEL // ADA_TN),
        in_specs=[
            pl.BlockSpec((8, D_MODEL), lambda l, j: (0, 0)),
            pl.BlockSpec((None, D_MODEL, ADA_TN), lambda l, j: (l, 0, j)),
            pl.BlockSpec((None, 1, ADA_TN), lambda l, j: (l, 0, j)),
        ],
        out_specs=pl.BlockSpec((None, 8, ADA_TN), lambda l, j: (l, 0, j)),
        out_shape=jax.ShapeDtypeStruct((DEPTH, 8, 6 * D_MODEL), F32),
        compiler_params=_params("arbitrary", "arbitrary"),
        name="ada_table",
    )(cond8, w_ada, b_ada.reshape(DEPTH, 1, 6 * D_MODEL))


def _inproj_kernel(x_ref, mod_ref, w_ref, wab_ref, ret_ref, s5_ref, gdn_ref, ab_ref):
    mod = mod_ref[...]
    sh1 = mod[:, 0:D_MODEL]
    sc1 = mod[:, D_MODEL:2 * D_MODEL]
    h = (_ln(x_ref[...]) * (1.0 + sc1) + sh1).astype(BF16)
    ret_ref[...] = _dot(h, w_ref[:, 0:1024])
    s5_ref[...] = _dot(h, w_ref[:, 1024:1280])
    gdn_ref[...] = _dot(h, w_ref[:, 1280:MAIN_COLS])
    ab_ref[...] = _dot(h, wab_ref[...])


def _in_proj(x_all, mod_l, w_main, w_ab, layer):
    tm = TOKEN_TILE
    return pl.pallas_call(
        _inproj_kernel,
        grid=(N_TILES,),
        in_specs=[
            pl.BlockSpec((tm, D_MODEL), lambda i: (i, 0)),
            pl.BlockSpec((None, 1, 6 * D_MODEL), lambda i: (_mod_row(i), 0, 0)),
            pl.BlockSpec((None, D_MODEL, MAIN_COLS), lambda i: (layer, 0, 0)),
            pl.BlockSpec((None, D_MODEL, LANES), lambda i: (layer, 0, 0)),
        ],
        out_specs=[
            pl.BlockSpec((tm, 1024), lambda i: (i, 0)),
            pl.BlockSpec((tm, 256), lambda i: (i, 0)),
            pl.BlockSpec((tm, 2048), lambda i: (i, 0)),
            pl.BlockSpec((tm, LANES), lambda i: (i, 0)),
        ],
        out_shape=[
            jax.ShapeDtypeStruct((T_ALL, 1024), F32),
            jax.ShapeDtypeStruct((T_ALL, 256), F32),
            jax.ShapeDtypeStruct((T_ALL, 2048), F32),
            jax.ShapeDtypeStruct((T_ALL, LANES), F32),
        ],
        compiler_params=_params("arbitrary"),
        name="in_proj",
    )(x_all, mod_l, w_main, w_ab)


RET_TQ = 256


def _ret_kernel(*refs, L, rope, has_init, emit_state):
    it = iter(refs)
    x_ref, dec_ref = next(it), next(it)
    cos_ref = next(it) if rope else None
    sin_ref = next(it) if rope else None
    r0_ref = next(it) if has_init else None
    y_ref = next(it)
    st_ref = next(it) if emit_state else None

    lg = -jnp.exp(dec_ref[...])
    q = x_ref[:, 0:256]
    k = x_ref[:, 256:512] * (RET_DK ** -0.5)
    v = x_ref[:, 512:768]
    if rope:
        lane = lax.broadcasted_iota(jnp.int32, (L, RET_WIDTH), 1)
        first_half = (lane % RET_DK) < (RET_DK // 2)
        cos, sin = cos_ref[...], sin_ref[...]

        def rot(z):
            swapped = jnp.where(first_half, pltpu.roll(z, RET_WIDTH - RET_DK // 2, axis=1),
                                pltpu.roll(z, RET_DK // 2, axis=1))
            return z * cos + swapped * sin

        q, k = rot(q), rot(k)
    kb16 = k.astype(BF16)
    vb16 = v.astype(BF16)
    tcol_all = lax.broadcasted_iota(jnp.int32, (L, 1), 0).astype(F32)

    for qb in range(L // RET_TQ):
        rows = slice(qb * RET_TQ, (qb + 1) * RET_TQ)
        t_i = lax.broadcasted_iota(jnp.int32, (RET_TQ, L), 0) + qb * RET_TQ
        s_i = lax.broadcasted_iota(jnp.int32, (RET_TQ, L), 1)
        dist = (t_i - s_i).astype(F32)
        diag = jnp.where(t_i == s_i, 1.0, 0.0)
        tcol = tcol_all[rows]
        pieces = []
        for h in range(RET_HEADS):
            cols = slice(h * RET_DK, (h + 1) * RET_DK)
            lgf = lg[0:1, h:h + 1]
            lgb = lg[1:2, h:h + 1]
            qh = q[rows, cols]
            scores = _dot_nt(qh.astype(BF16), kb16[:, cols])
            decay = jnp.exp(jnp.where(dist >= 0, lgf * dist, -lgb * dist)) + diag
            o = _dot((scores * decay).astype(BF16), vb16[:, cols])
            if has_init:
                qf = (qh * jnp.exp(lgf * (tcol + 1.0))).astype(BF16)
                qr = (qh * jnp.exp(lgb * (float(L) - tcol))).astype(BF16)
                o = o + _dot(qf, r0_ref[0, h].astype(BF16)) + _dot(qr, r0_ref[1, h].astype(BF16))
            pieces.append(_ln(o) * _silu(x_ref[rows, 768 + h * RET_DK:768 + (h + 1) * RET_DK]))
        y_ref[rows, :] = jnp.concatenate(pieces, axis=1).astype(BF16)

    if emit_state:
        for h in range(RET_HEADS):
            cols = slice(h * RET_DK, (h + 1) * RET_DK)
            wf = jnp.exp(lg[0:1, h:h + 1] * (float(L - 1) - tcol_all))
            wb = jnp.exp(lg[1:2, h:h + 1] * tcol_all)
            st_ref[0, h] = _dot_tn((k[:, cols] * wf).astype(BF16), vb16[:, cols])
            st_ref[1, h] = _dot_tn((k[:, cols] * wb).astype(BF16), vb16[:, cols])


def _retention(ret_in, ret_decay_l, layer, *, latent, rope_tabs=None, state=None):
    L = L_LAT if latent else L_CTX
    nb = N_LAT if latent else N_CTX
    off = T_CTX // L if latent else 0
    kern = functools.partial(_ret_kernel, L=L, rope=latent, has_init=latent, emit_state=not latent)
    in_specs = [pl.BlockSpec((L, 1024), lambda b: (b + off, 0)),
                pl.BlockSpec((2, RET_HEADS), lambda b: (0, 0))]
    args = [ret_in, ret_decay_l]
    if latent:
        in_specs += [pl.BlockSpec((L, RET_WIDTH), lambda b: (0, 0))] * 2
        args += list(rope_tabs)
        in_specs.append(pl.BlockSpec((None, None, 2, RET_HEADS, RET_DK, RET_DK),
                                     lambda b: (b, layer, 0, 0, 0, 0)))
        args.append(state)
    out_specs = [pl.BlockSpec((L, RET_WIDTH), lambda b: (b, 0))]
    out_shape = [jax.ShapeDtypeStruct((nb * L, RET_WIDTH), BF16)]
    if not latent:
        out_specs.append(pl.BlockSpec((None, 2, RET_HEADS, RET_DK, RET_DK), lambda b: (b, 0, 0, 0, 0)))
        out_shape.append(jax.ShapeDtypeStruct((nb, 2, RET_HEADS, RET_DK, RET_DK), F32))
    return pl.pallas_call(
        kern, grid=(nb,), in_specs=in_specs, out_specs=out_specs, out_shape=out_shape,
        compiler_params=_params("arbitrary"),
        name="retention_lat" if latent else "retention_ctx",
    )(*args)


def _s5_prep_kernel(are_ref, aim_ref, ldt_ref, bre_ref, bim_ref, abar_ref, wb_ref):
    are, aim = are_ref[...], aim_ref[...]
    dt = jnp.exp(ldt_ref[...])
    mag = jnp.exp(are * dt)
    abr = mag * jnp.cos(aim * dt)
    abi = mag * jnp.sin(aim * dt)
    den = are * are + aim * aim
    cr = ((abr - 1.0) * are + abi * aim) / den
    ci = (abi * are - (abr - 1.0) * aim) / den
    bre, bim = bre_ref[...], bim_ref[...]
    for d in range(2):
        crd, cid = cr[d:d + 1], ci[d:d + 1]
        wb_ref[d, :, 0:S5_STATE] = (crd * bre - cid * bim).astype(BF16)
        wb_ref[d, :, S5_STATE:2 * S5_STATE] = (crd * bim + cid * bre).astype(BF16)
        abar_ref[2 * d:2 * d + 1, :] = abr[d:d + 1]
        abar_ref[2 * d + 1:2 * d + 2, :] = abi[d:d + 1]


def _s5_prep(a_re, a_im, ldt_rep, bre_bd, bim_bd):
    vec = pl.BlockSpec((None, 2, S5_STATE), lambda l: (l, 0, 0))
    mat = pl.BlockSpec((None, S5_WIDTH, S5_STATE), lambda l: (l, 0, 0))
    return pl.pallas_call(
        _s5_prep_kernel,
        grid=(DEPTH,),
        in_specs=[vec, vec, vec, mat, mat],
        out_specs=[pl.BlockSpec((None, 4, S5_STATE), lambda l: (l, 0, 0)),
                   pl.BlockSpec((None, 2, S5_WIDTH, 2 * S5_STATE), lambda l: (l, 0, 0, 0))],
        out_shape=[jax.ShapeDtypeStruct((DEPTH, 4, S5_STATE), F32),
                   jax.ShapeDtypeStruct((DEPTH, 2, S5_WIDTH, 2 * S5_STATE), BF16)],
        compiler_params=_params("arbitrary"),
        name="s5_prep",
    )(a_re, a_im, ldt_rep, bre_bd, bim_bd)


def _gelu_tanh(x):
    return x * (0.5 * (1.0 + jnp.tanh(np.sqrt(2.0 / np.pi).astype(np.float32) * (x + 0.044715 * (x * x * x)))))


def _s5_kernel(*refs, L, has_init, emit_state):
    it = iter(refs)
    u_ref, abar_ref, wb_ref, wcre_ref, wcim_ref, d_ref, wglu_ref = (next(it) for _ in range(7))
    x0re_ref = next(it) if has_init else None
    x0im_ref = next(it) if has_init else None
    y_ref = next(it)
    fre_ref = next(it) if emit_state else None
    fim_ref = next(it) if emit_state else None
    sf_ref, sb_ref = next(it), next(it)

    u = u_ref[...]
    ub = u.astype(BF16)
    sf_ref[...] = _dot(ub, wb_ref[0])
    sb_ref[...] = _dot(ub, wb_ref[1])
    afr, afi = abar_ref[0:1, :], abar_ref[1:2, :]
    abr, abi = abar_ref[2:3, :], abar_ref[3:4, :]
    if has_init:
        init = (x0re_ref[0:1, :], x0im_ref[0:1, :], x0re_ref[1:2, :], x0im_ref[1:2, :])
    else:
        init = tuple(jnp.zeros((1, S5_STATE), F32) for _ in range(4))

    def body(t, carry):
        xfr, xfi, xbr, xbi = carry
        tb = L - 1 - t
        nfr = afr * xfr - afi * xfi + sf_ref[pl.ds(t, 1), 0:S5_STATE]
        nfi = afr * xfi + afi * xfr + sf_ref[pl.ds(t, 1), S5_STATE:2 * S5_STATE]
        nbr = abr * xbr - abi * xbi + sb_ref[pl.ds(tb, 1), 0:S5_STATE]
        nbi = abr * xbi + abi * xbr + sb_ref[pl.ds(tb, 1), S5_STATE:2 * S5_STATE]
        sf_ref[pl.ds(t, 1), 0:S5_STATE] = nfr
        sf_ref[pl.ds(t, 1), S5_STATE:2 * S5_STATE] = nfi
        sb_ref[pl.ds(tb, 1), 0:S5_STATE] = nbr
        sb_ref[pl.ds(tb, 1), S5_STATE:2 * S5_STATE] = nbi
        return nfr, nfi, nbr, nbi

    xfr, xfi, xbr, xbi = lax.fori_loop(0, L, body, init, unroll=4)
    if emit_state:
        fre_ref[0:1, :] = xfr
        fre_ref[1:2, :] = xbr
        fim_ref[0:1, :] = xfi
        fim_ref[1:2, :] = xbi

    states = (sf_ref[...] + sb_ref[...]).astype(BF16)
    y = _dot(states[:, 0:S5_STATE], wcre_ref[...]) - _dot(states[:, S5_STATE:2 * S5_STATE], wcim_ref[...])
    y = _gelu_tanh(y + d_ref[...] * u)
    y_ref[...] = (y * jax.nn.sigmoid(_dot(y.astype(BF16), wglu_ref[...]))).astype(BF16)


def _s5(s5_in, abar_l, wb_l, wcre_l, wcim_l, d_l, wglu_l, layer, *, latent, x0re=None, x0im=None):
    L = L_LAT if latent else L_CTX
    nb = N_LAT if latent else N_CTX
    off = T_CTX // L if latent else 0
    kern = functools.partial(_s5_kernel, L=L, has_init=latent, emit_state=not latent)
    const2 = lambda b: (0, 0)
    in_specs = [pl.BlockSpec((L, S5_WIDTH), lambda b: (b + off, 0)),
                pl.BlockSpec((4, S5_STATE), const2),
                pl.BlockSpec((2, S5_WIDTH, 2 * S5_STATE), lambda b: (0, 0, 0)),
                pl.BlockSpec((S5_STATE, S5_WIDTH), const2),
                pl.BlockSpec((S5_STATE, S5_WIDTH), const2),
                pl.BlockSpec((1, S5_WIDTH), const2),
                pl.BlockSpec((S5_WIDTH, S5_WIDTH), const2)]
    args = [s5_in, abar_l, wb_l, wcre_l, wcim_l, d_l, wglu_l]
    if latent:
        st = pl.BlockSpec((None, None, 2, S5_STATE), lambda b: (b, layer, 0, 0))
        in_specs += [st, st]
        args += [x0re, x0im]
    out_specs = [pl.BlockSpec((L, S5_WIDTH), lambda b: (b, 0))]
    out_shape = [jax.ShapeDtypeStruct((nb * L, S5_WIDTH), BF16)]
    if not latent:
        fs = pl.BlockSpec((None, 2, S5_STATE), lambda b: (b, 0, 0))
        out_specs += [fs, fs]
        out_shape += [jax.ShapeDtypeStruct((nb, 2, S5_STATE), F32)] * 2
    return pl.pallas_call(
        kern, grid=(nb,), in_specs=in_specs, out_specs=out_specs, out_shape=out_shape,
        scratch_shapes=[pltpu.VMEM((L, 2 * S5_STATE), F32), pltpu.VMEM((L, 2 * S5_STATE), F32)],
        compiler_params=_params("arbitrary"),
        name="s5_lat" if latent else "s5_ctx",
    )(*args)


def _softplus(x):
    return jnp.maximum(x, 0.0) + jnp.log1p(jnp.exp(-jnp.abs(x)))


def _split(x):
    hi = x.astype(BF16)
    return hi, (x - hi.astype(F32)).astype(BF16)


GDN_PACK = 4
PACK_W = GDN_PACK * GDN_DK
N_PACKS = GDN_HEADS // GDN_PACK


def _split3(x):
    hi = x.astype(BF16)
    r = x - hi.astype(F32)
    mid = r.astype(BF16)
    return hi, mid, (r - mid.astype(F32)).astype(BF16)


def _dot_exact_lhs(a16, x):
    hi, mid, lo = _split3(x)
    return _dot(a16, hi) + _dot(a16, mid) + _dot(a16, lo)


def _gdn_packed_kernel(*refs, L, has_init, emit_state):
    it = iter(refs)
    x_ref, ab_ref, conv_ref, alog_ref, dtb_ref, norm_ref = (next(it) for _ in range(6))
    s0_ref = next(it) if has_init else None
    y_ref = next(it)
    sfin_ref = next(it) if emit_state else None
    q_s, k_s, v_s, gfx_s, gbx_s, betax_s, of_s, ob_s, st_s = (next(it) for _ in range(9))
    n = L // CHUNK
    qkv = 3 * GDN_WIDTH

    w0, w1, w2 = conv_ref[0:1, :], conv_ref[1:2, :], conv_ref[2:3, :]
    rate_f, rate_b = jnp.exp(alog_ref[0:1, :]), jnp.exp(alog_ref[1:2, :])
    dtb_f, dtb_b = dtb_ref[0:1, :], dtb_ref[1:2, :]
    row_q = lax.broadcasted_iota(jnp.int32, (CHUNK, qkv), 0)
    e_row = lax.broadcasted_iota(jnp.int32, (LANES, 2 * GDN_WIDTH), 0)
    e_col = lax.broadcasted_iota(jnp.int32, (LANES, 2 * GDN_WIDTH), 1)
    expand = jnp.where(e_row == jnp.where(e_col < GDN_WIDTH, e_col >> 6, GDN_HEADS + ((e_col - GDN_WIDTH) >> 6)),
                       1.0, 0.0).astype(BF16)

    def prep(c, _):
        r0 = pl.multiple_of(c * CHUNK, CHUNK)
        xc = x_ref[pl.ds(r0, CHUNK), 0:qkv]
        prev = x_ref[pl.ds(jnp.maximum(r0 - 1, 0), 1), 0:qkv] * (c > 0).astype(F32)
        nxt = x_ref[pl.ds(jnp.minimum(r0 + CHUNK, L - 1), 1), 0:qkv] * (c < n - 1).astype(F32)
        xm1 = jnp.where(row_q == 0, prev, pltpu.roll(xc, 1, axis=0))
        xp1 = jnp.where(row_q == CHUNK - 1, nxt, pltpu.roll(xc, CHUNK - 1, axis=0))
        y = _silu(w0 * xm1 + w1 * xc + w2 * xp1)
        qs, ks = [], []
        for h in range(GDN_HEADS):
            qh = y[:, h * GDN_DK:(h + 1) * GDN_DK]
            kh = y[:, GDN_WIDTH + h * GDN_DK:GDN_WIDTH + (h + 1) * GDN_DK]
            qs.append(qh * lax.rsqrt(jnp.sum(qh * qh, -1, keepdims=True) + 1e-6) * (GDN_DK ** -0.5))
            ks.append(kh * lax.rsqrt(jnp.sum(kh * kh, -1, keepdims=True) + 1e-6))
        q_s[pl.ds(r0, CHUNK), :] = jnp.concatenate(qs, axis=1)
        k_s[pl.ds(r0, CHUNK), :] = jnp.concatenate(ks, axis=1)
        v_s[pl.ds(r0, CHUNK), :] = y[:, 2 * GDN_WIDTH:qkv]
        ab = ab_ref[pl.ds(r0, CHUNK), :]
        gates = jnp.concatenate([-rate_f * _softplus(ab + dtb_f), -rate_b * _softplus(ab + dtb_b),
                                 jax.nn.sigmoid(ab)], axis=0)
        hi, mid, lo = _split3(gates)
        wide = _dot(hi, expand) + _dot(mid, expand) + _dot(lo, expand)
        gfx_s[pl.ds(r0, CHUNK), :] = wide[0:CHUNK, 0:GDN_WIDTH]
        gbx_s[pl.ds(r0, CHUNK), :] = wide[CHUNK:2 * CHUNK, 0:GDN_WIDTH]
        betax_s[pl.ds(r0, CHUNK), :] = wide[2 * CHUNK:3 * CHUNK, GDN_WIDTH:2 * GDN_WIDTH]
        return 0

    lax.fori_loop(0, n, prep, 0)

    for d in range(2):
        for h in range(GDN_HEADS):
            cols = slice(h * GDN_DK, (h + 1) * GDN_DK)
            st_s[d, :, cols] = s0_ref[d, h] if has_init else jnp.zeros((GDN_DK, GDN_DK), F32)

    row_w = lax.broadcasted_iota(jnp.int32, (CHUNK, GDN_WIDTH), 0)
    lane_w = lax.broadcasted_iota(jnp.int32, (CHUNK, GDN_WIDTH), 1)
    col_w = lane_w & (GDN_DK - 1)
    eye_w = jnp.where(row_w == col_w, 1.0, 0.0)
    row_p = lax.broadcasted_iota(jnp.int32, (CHUNK, PACK_W), 0)
    lane_p = lax.broadcasted_iota(jnp.int32, (CHUNK, PACK_W), 1)
    eye_p = jnp.where(row_p == (lane_p & (GDN_DK - 1)), 1.0, 0.0)
    head_p = lane_p >> 6
    bd_row = lax.broadcasted_iota(jnp.int32, (PACK_W, PACK_W), 0) >> 6
    bd_col = lax.broadcasted_iota(jnp.int32, (PACK_W, PACK_W), 1) >> 6
    bd_mask = bd_row == bd_col
    r64 = lax.broadcasted_iota(jnp.int32, (CHUNK, CHUNK), 0)
    c64 = lax.broadcasted_iota(jnp.int32, (CHUNK, CHUNK), 1)
    ones16 = jnp.ones((CHUNK, CHUNK), BF16)
    zero16 = jnp.zeros((), BF16)

    def bd(x16):
        return jnp.where(bd_mask, jnp.concatenate([x16] * GDN_PACK, axis=0), zero16)

    def step(i, _):
        chains = []
        for d in range(2):
            c = i if d == 0 else n - 1 - i
            r0 = pl.multiple_of(c * CHUNK, CHUNK)
            incl64 = (r64 >= c64) if d == 0 else (r64 <= c64)
            incl = (row_w >= col_w) if d == 0 else (row_w <= col_w)
            strict = (row_w > col_w) if d == 0 else (row_w < col_w)
            last = CHUNK - 1 if d == 0 else 0
            gx = (gfx_s if d == 0 else gbx_s)[pl.ds(r0, CHUNK), :]
            gc = _dot_exact_lhs(jnp.where(incl64, 1.0, 0.0).astype(BF16), gx)
            grow = _dot_exact_lhs(ones16, gc * eye_w)
            dmat = jnp.where(incl, jnp.exp(jnp.where(incl, gc - grow, 0.0)), 0.0)
            eg = jnp.exp(gc)
            glast = gc[last:last + 1, :]
            beta = betax_s[pl.ds(r0, CHUNK), :]
            qc = q_s[pl.ds(r0, CHUNK), :]
            kc = k_s[pl.ds(r0, CHUNK), :]
            vc = v_s[pl.ds(r0, CHUNK), :]
            kbeta = kc * beta
            vbeta = vc * beta
            k_dec = kc * jnp.exp(glast - gc)
            s_decay = jnp.exp(glast)
            s_all = st_s[d]
            for p in range(N_PACKS):
                cols = slice(p * PACK_W, (p + 1) * PACK_W)
                chains.append(dict(
                    d=d, p_idx=p, incl=incl[:, cols], strict=strict[:, cols], dmat=dmat[:, cols], q=qc[:, cols],
                    k=kc[:, cols], kbeta=kbeta[:, cols], vbeta=vbeta[:, cols], eg=eg[:, cols],
                    k_dec=k_dec[:, cols], s_decay=s_decay[:, cols], s=s_all[:, cols]))

        for ch in chains:
            lhs = jnp.concatenate([ch["kbeta"], ch["q"]], axis=0).astype(BF16)
            both = _dot_nt(lhs, bd(ch["k"].astype(BF16)))
            ch["p"] = -jnp.where(ch["strict"], both[0:CHUNK] * ch["dmat"], 0.0)
            ch["attn"] = jnp.where(ch["incl"], both[CHUNK:2 * CHUNK] * ch["dmat"], 0.0)
            ch["t"] = eye_p + ch["p"]

        for stage in range(6):
            for ch in chains:
                p_hi, p_lo = _split(ch["p"])
                rhs_hi, rhs_lo = bd(p_hi), bd(p_lo)
                parts_hi, parts_lo = [], []
                if stage > 0:
                    t_hi, t_lo = _split(ch["t"])
                    parts_hi += [t_hi, t_lo]
                    parts_lo += [t_hi]
                if stage < 5:
                    parts_hi += [p_hi, p_lo]
                    parts_lo += [p_hi]
                out_hi = _dot(jnp.concatenate(parts_hi, axis=0), rhs_hi)
                out_lo = _dot(jnp.concatenate(parts_lo, axis=0) if len(parts_lo) > 1 else parts_lo[0], rhs_lo)
                at = 0
                if stage > 0:
                    ch["t"] = ch["t"] + (out_hi[0:CHUNK] + out_hi[CHUNK:2 * CHUNK] + out_lo[0:CHUNK])
                    at = 1
                if stage < 5:
                    ch["p"] = (out_hi[2 * at * CHUNK:(2 * at + 1) * CHUNK] + out_hi[(2 * at + 1) * CHUNK:(2 * at + 2) * CHUNK]
                               + out_lo[at * CHUNK:(at + 1) * CHUNK])

        outs = [[None] * N_PACKS, [None] * N_PACKS]
        new_s = [[None] * N_PACKS, [None] * N_PACKS]
        for ch in chains:
            lhs = jnp.concatenate([ch["kbeta"] * ch["eg"], ch["q"] * ch["eg"]], axis=0).astype(BF16)
            both = _dot(lhs, bd(ch["s"].astype(BF16)))
            r_hi, r_lo = _split(ch["vbeta"] - both[0:CHUNK])
            t_hi, t_lo = _split(ch["t"])
            tr = _dot(jnp.concatenate([t_hi, t_lo], axis=0), bd(r_hi))
            v_new = tr[0:CHUNK] + tr[CHUNK:2 * CHUNK] + _dot(t_hi, bd(r_lo))
            vn16 = v_new.astype(BF16)
            outs[ch["d"]][ch["p_idx"]] = both[CHUNK:2 * CHUNK] + _dot(ch["attn"].astype(BF16), bd(vn16))
            upd = _dot_tn(ch["k_dec"].astype(BF16), vn16)
            s_new = ch["s"] * ch["s_decay"]
            for h in range(GDN_PACK):
                s_new = s_new + jnp.where(head_p == h, upd[h * GDN_DK:(h + 1) * GDN_DK, :], 0.0)
            new_s[ch["d"]][ch["p_idx"]] = s_new

        for d in range(2):
            st_s[d] = jnp.concatenate(new_s[d], axis=1)
        of_s[pl.ds(pl.multiple_of(i * CHUNK, CHUNK), CHUNK), :] = jnp.concatenate(outs[0], axis=1)
        ob_s[pl.ds(pl.multiple_of((n - 1 - i) * CHUNK, CHUNK), CHUNK), :] = jnp.concatenate(outs[1], axis=1)
        return 0

    lax.fori_loop(0, n, step, 0)
    if emit_state:
        for d in range(2):
            for h in range(GDN_HEADS):
                sfin_ref[d, h] = st_s[d, :, h * GDN_DK:(h + 1) * GDN_DK]

    gain = norm_ref[...]

    def fin(c, _):
        r0 = pl.multiple_of(c * CHUNK, CHUNK)
        o = of_s[pl.ds(r0, CHUNK), :] + ob_s[pl.ds(r0, CHUNK), :]
        z = x_ref[pl.ds(r0, CHUNK), qkv:qkv + GDN_WIDTH]
        pieces = []
        for h in range(GDN_HEADS):
            cols = slice(h * GDN_DK, (h + 1) * GDN_DK)
            oh = o[:, cols]
            on = oh * lax.rsqrt(jnp.mean(oh * oh, -1, keepdims=True) + LN_EPS) * gain
            pieces.append(on * _silu(z[:, cols]))
        y_ref[pl.ds(r0, CHUNK), :] = jnp.concatenate(pieces, axis=1).astype(BF16)
        return 0

    lax.fori_loop(0, n, fin, 0)


def _gdn(gdn_in, ab_in, conv_l, alog_l, dtb_l, norm_l, layer, *, latent, state=None):
    L = L_LAT if latent else L_CTX
    nb = N_LAT if latent else N_CTX
    off = T_CTX // L if latent else 0
    kern = functools.partial(_gdn_packed_kernel, L=L, has_init=latent, emit_state=not latent)
    const2 = lambda b: (0, 0)
    in_specs = [pl.BlockSpec((L, 2048), lambda b: (b + off, 0)),
                pl.BlockSpec((L, LANES), lambda b: (b + off, 0)),
                pl.BlockSpec((3, 3 * GDN_WIDTH), const2),
                pl.BlockSpec((2, LANES), const2),
                pl.BlockSpec((2, LANES), const2),
                pl.BlockSpec((1, GDN_DK), const2)]
    args = [gdn_in, ab_in, conv_l, alog_l, dtb_l, norm_l]
    if latent:
        in_specs.append(pl.BlockSpec((None, None, 2, GDN_HEADS, GDN_DK, GDN_DK),
                                     lambda b: (b, layer, 0, 0, 0, 0)))
        args.append(state)
    out_specs = [pl.BlockSpec((L, GDN_WIDTH), lambda b: (b, 0))]
    out_shape = [jax.ShapeDtypeStruct((nb * L, GDN_WIDTH), BF16)]
    if not latent:
        out_specs.append(pl.BlockSpec((None, 2, GDN_HEADS, GDN_DK, GDN_DK), lambda b: (b, 0, 0, 0, 0)))
        out_shape.append(jax.ShapeDtypeStruct((nb, 2, GDN_HEADS, GDN_DK, GDN_DK), F32))
    wide = pltpu.VMEM((L, GDN_WIDTH), F32)
    return pl.pallas_call(
        kern, grid=(nb,), in_specs=in_specs, out_specs=out_specs, out_shape=out_shape,
        scratch_shapes=[wide] * 8 + [pltpu.VMEM((2, GDN_DK, GDN_WIDTH), F32)],
        compiler_params=_params("arbitrary"),
        name="gdn_lat" if latent else "gdn_ctx",
    )(*args)


def _post_kernel(x_ref, yrc_ref, yrl_ref, ysc_ref, ysl_ref, ygc_ref, ygl_ref, mod_ref, wout_ref, g_ref, b_ref,
                 wr_ref, br_ref, x1_ref, h2_ref, gate_ref):
    is_ctx = pl.program_id(0) < CTX_TILES
    yr = jnp.where(is_ctx, yrc_ref[...], yrl_ref[...])
    ys = jnp.where(is_ctx, ysc_ref[...], ysl_ref[...])
    yg = jnp.where(is_ctx, ygc_ref[...], ygl_ref[...])
    mix = _dot(yr, wout_ref[0:256, :]) + _dot(ys, wout_ref[256:512, :]) + _dot(yg, wout_ref[512:1024, :])
    mod = mod_ref[...]
    g1 = mod[:, 2 * D_MODEL:3 * D_MODEL]
    sh2 = mod[:, 3 * D_MODEL:4 * D_MODEL]
    sc2 = mod[:, 4 * D_MODEL:5 * D_MODEL]
    x1 = _ln(ALPHA * x_ref[...] + g1 * mix) * g_ref[...] + b_ref[...]
    x1_ref[...] = x1
    h2 = _ln(x1) * (1.0 + sc2) + sh2
    h2_ref[...] = h2.astype(BF16)

    logits = _dot_f32(h2, wr_ref[...]) + br_ref[...]
    lane = lax.broadcasted_iota(jnp.int32, logits.shape, 1)
    lane_f = lane.astype(F32)
    big = float(LANES)
    neg = -jnp.inf
    is_g = (lane >= N_EXPERTS) & (lane < N_EXPERTS + N_EGROUPS)
    gl = jnp.where(is_g, logits, neg)
    gmax = jnp.max(gl, -1, keepdims=True)
    g_sel = jnp.min(jnp.where(is_g & (gl == gmax), lane_f, big), -1, keepdims=True) - float(N_EXPERTS)
    p_group = 1.0 / jnp.sum(jnp.where(is_g, jnp.exp(gl - gmax), 0.0), -1, keepdims=True)
    in_sel = (lane < N_EXPERTS) & ((lane >> 2).astype(F32) == g_sel)
    el = jnp.where(in_sel, logits, neg)
    emax = jnp.max(el, -1, keepdims=True)
    ee = jnp.where(in_sel, jnp.exp(el - emax), 0.0)
    prob = ee / jnp.sum(ee, -1, keepdims=True)
    p1 = jnp.max(jnp.where(in_sel, prob, -1.0), -1, keepdims=True)
    i1 = jnp.min(jnp.where(in_sel & (prob == p1), lane_f, big), -1, keepdims=True)
    rest = in_sel & (lane_f != i1)
    p2 = jnp.max(jnp.where(rest, prob, -1.0), -1, keepdims=True)
    i2 = jnp.min(jnp.where(rest & (prob == p2), lane_f, big), -1, keepdims=True)
    den = p1 + p2
    gate_ref[...] = jnp.where(lane_f == i1, p_group * p1 / den,
                              jnp.where(lane_f == i2, p_group * p2 / den, 0.0))


def _post(x_all, y_ret, y_s5, y_gdn, mod_l, w_out16, layer, ln_g, ln_b, wr_l, br_l):
    tm = TOKEN_TILE
    const2 = lambda i: (0, 0)
    ctx_tile = lambda i: (jnp.minimum(i, CTX_TILES - 1), 0)
    lat_tile = lambda i: (jnp.maximum(i - CTX_TILES, 0), 0)
    pair = lambda w: [pl.BlockSpec((tm, w), ctx_tile), pl.BlockSpec((tm, w), lat_tile)]
    return pl.pallas_call(
        _post_kernel,
        grid=(N_TILES,),
        in_specs=[
            pl.BlockSpec((tm, D_MODEL), lambda i: (i, 0)),
            *pair(RET_WIDTH), *pair(S5_WIDTH), *pair(GDN_WIDTH),
            pl.BlockSpec((None, 1, 6 * D_MODEL), lambda i: (_mod_row(i), 0, 0)),
            pl.BlockSpec((None, D_MODEL, D_MODEL), lambda i: (layer, 0, 0)),
            pl.BlockSpec((1, D_MODEL), const2),
            pl.BlockSpec((1, D_MODEL), const2),
            pl.BlockSpec((D_MODEL, LANES), const2),
            pl.BlockSpec((1, LANES), const2),
        ],
        out_specs=[
            pl.BlockSpec((tm, D_MODEL), lambda i: (i, 0)),
            pl.BlockSpec((tm, D_MODEL), lambda i: (i, 0)),
            pl.BlockSpec((tm, LANES), lambda i: (i, 0)),
        ],
        out_shape=[
            jax.ShapeDtypeStruct((T_ALL, D_MODEL), F32),
            jax.ShapeDtypeStruct((T_ALL, D_MODEL), BF16),
            jax.ShapeDtypeStruct((T_ALL, LANES), F32),
        ],
        compiler_params=_params("arbitrary"),
        name="post_mix",
    )(x_all, *y_ret, *y_s5, *y_gdn, mod_l, w_out16, ln_g, ln_b, wr_l, br_l)


def _moe_kernel(h_ref, gate_ref, x1_ref, mod_ref, wg_ref, wu_ref, wd_ref, g_ref, b_ref, o_ref, acc_ref):
    e = pl.program_id(1)

    @pl.when(e == 0)
    def _():
        acc_ref[...] = jnp.zeros(acc_ref.shape, F32)

    h = h_ref[...]
    gate = gate_ref[...]
    lane = lax.broadcasted_iota(jnp.int32, gate.shape, 1)
    ge = jnp.sum(jnp.where(lane == e, gate, 0.0), -1, keepdims=True)
    act = _silu(_dot(h, wg_ref[...])) * _dot(h, wu_ref[...]) * ge
    acc_ref[...] += _dot(act.astype(BF16), wd_ref[...])

    @pl.when(e == N_EXPERTS - 1)
    def _():
        g2 = mod_ref[:, 5 * D_MODEL:6 * D_MODEL]
        o_ref[...] = _ln(ALPHA * x1_ref[...] + g2 * acc_ref[...]) * g_ref[...] + b_ref[...]


def _moe(h2, gate, x1, mod_l, wg16, wu16, wd16, layer, ln_g, ln_b):
    tm = TOKEN_TILE
    const2 = lambda i, e: (0, 0)
    expert = lambda i, e: (layer, e, 0, 0)
    return pl.pallas_call(
        _moe_kernel,
        grid=(N_TILES, N_EXPERTS),
        in_specs=[
            pl.BlockSpec((tm, D_MODEL), lambda i, e: (i, 0)),
            pl.BlockSpec((tm, LANES), lambda i, e: (i, 0)),
            pl.BlockSpec((tm, D_MODEL), lambda i, e: (i, 0)),
            pl.BlockSpec((None, 1, 6 * D_MODEL), lambda i, e: (_mod_row(i), 0, 0)),
            pl.BlockSpec((None, None, D_MODEL, D_FF), expert),
            pl.BlockSpec((None, None, D_MODEL, D_FF), expert),
            pl.BlockSpec((None, None, D_FF, D_MODEL), expert),
            pl.BlockSpec((1, D_MODEL), const2),
            pl.BlockSpec((1, D_MODEL), const2),
        ],
        out_specs=pl.BlockSpec((tm, D_MODEL), lambda i, e: (i, 0)),
        out_shape=jax.ShapeDtypeStruct((T_ALL, D_MODEL), F32),
        scratch_shapes=[pltpu.VMEM((tm, D_MODEL), F32)],
        compiler_params=_params("arbitrary", "arbitrary"),
        name="moe",
    )(h2, gate, x1, mod_l, wg16, wu16, wd16, ln_g, ln_b)


def _rope_tables():
    rows = L_LAT // GRID_W
    row = jnp.repeat(jnp.arange(rows, dtype=F32), GRID_W)
    col = (jnp.arange(rows * GRID_W) % GRID_W).astype(F32)
    n_freq = RET_DK // 4
    inv_freq = ROPE_BASE ** (-jnp.arange(n_freq, dtype=F32) / n_freq)
    ang = jnp.concatenate([row[:, None] * inv_freq, col[:, None] * inv_freq], -1)
    cos, sin = jnp.cos(ang), jnp.sin(ang)
    cos_t = jnp.tile(jnp.concatenate([cos, cos], -1), (1, RET_HEADS))
    sin_t = jnp.tile(jnp.concatenate([-sin, sin], -1), (1, RET_HEADS))
    return cos_t, sin_t


def _block_diag_in(b):
    eye = jnp.eye(S5_GROUPS, dtype=F32)
    bt = jnp.transpose(b, (0, 1, 3, 2))
    return (bt[:, :, :, None, :] * eye[None, :, None, :, None]).reshape(DEPTH, S5_WIDTH, S5_STATE)


def _block_diag_out(c):
    eye = jnp.eye(S5_GROUPS, dtype=F32)
    ct = jnp.transpose(c, (0, 1, 3, 2))
    return (ct[:, :, :, None, :] * eye[None, :, None, :, None]).reshape(DEPTH, S5_STATE, S5_WIDTH)


def _pad_lanes(a):
    return jnp.pad(a, [(0, 0)] * (a.ndim - 1) + [(0, LANES - a.shape[-1])])


def kernel(x_prompt, x_sample, state_ret, state_s5_re, state_s5_im, state_gdn, c, c_ctx, w_ada, b_ada, w_in, ret_decay, s5_a_re, s5_a_im, s5_log_dt, s5_b_re, s5_b_im, s5_c_re, s5_c_im, s5_d, s5_w_glu, gdn_conv, gdn_a_log, gdn_dt_bias, gdn_norm, w_out, ln1_g, ln1_b, ln2_g, ln2_b, router_group_w, router_group_b, router_expert_w, router_expert_b, w_gate, w_up, w_down):
    x_all = jnp.concatenate([x_prompt.reshape(T_CTX, D_MODEL), x_sample.reshape(T_LAT, D_MODEL)], 0)
    cond8 = jnp.concatenate([c, c_ctx[None, :], jnp.zeros((8 - N_LAT - 1, D_MODEL), F32)], 0)
    mod = _ada_table(cond8, w_ada, b_ada).reshape(DEPTH, 8, 1, 6 * D_MODEL)

    w_main = w_in[:, :, :MAIN_COLS].astype(BF16)
    w_ab = _pad_lanes(w_in[:, :, MAIN_COLS:]).astype(BF16)
    w_out16 = w_out.astype(BF16)
    wg16, wu16, wd16 = w_gate.astype(BF16), w_up.astype(BF16), w_down.astype(BF16)
    wr = _pad_lanes(jnp.concatenate([router_expert_w, router_group_w], -1))
    br = _pad_lanes(jnp.concatenate([router_expert_b, router_group_b], -1)).reshape(DEPTH, 1, LANES)
    rope_tabs = _rope_tables()

    abar, wb = _s5_prep(s5_a_re.reshape(DEPTH, 2, S5_STATE), s5_a_im.reshape(DEPTH, 2, S5_STATE),
                        jnp.repeat(s5_log_dt, S5_P, axis=-1),
                        _block_diag_in(s5_b_re), _block_diag_in(s5_b_im))
    wcre = _block_diag_out(s5_c_re).astype(BF16)
    wcim = _block_diag_out(s5_c_im).astype(BF16)
    wglu16 = s5_w_glu.astype(BF16)
    x0re = state_s5_re.reshape(N_LAT, DEPTH, 2, S5_STATE)
    x0im = state_s5_im.reshape(N_LAT, DEPTH, 2, S5_STATE)
    alog = _pad_lanes(gdn_a_log)
    dtb = _pad_lanes(gdn_dt_bias)

    ret_l, s5re_l, s5im_l, gdn_l = [], [], [], []
    for l in range(DEPTH):
        ret_in, s5_in, gdn_in, ab_in = _in_proj(x_all, mod[l], w_main, w_ab, l)

        yr_c, st_ret = _retention(ret_in, ret_decay[l], l, latent=False)
        (yr_l,) = _retention(ret_in, ret_decay[l], l, latent=True, rope_tabs=rope_tabs, state=state_ret)
        s5_args = (abar[l], wb[l], wcre[l], wcim[l], s5_d[l].reshape(1, S5_WIDTH), wglu16[l], l)
        ys_c, fre, fim = _s5(s5_in, *s5_args, latent=False)
        (ys_l,) = _s5(s5_in, *s5_args, latent=True, x0re=x0re, x0im=x0im)
        gdn_args = (gdn_conv[l], alog[l], dtb[l], gdn_norm[l].reshape(1, GDN_DK), l)
        yg_c, st_gdn = _gdn(gdn_in, ab_in, *gdn_args, latent=False)
        (yg_l,) = _gdn(gdn_in, ab_in, *gdn_args, latent=True, state=state_gdn)

        x1, h2, gate = _post(x_all, (yr_c, yr_l), (ys_c, ys_l), (yg_c, yg_l), mod[l], w_out16, l,
                             ln1_g[l].reshape(1, D_MODEL), ln1_b[l].reshape(1, D_MODEL), wr[l], br[l])
        x_all = _moe(h2, gate, x1, mod[l], wg16, wu16, wd16, l,
                     ln2_g[l].reshape(1, D_MODEL), ln2_b[l].reshape(1, D_MODEL))

        ret_l.append(st_ret)
        s5re_l.append(fre.reshape(N_CTX, 2, S5_GROUPS, S5_P))
        s5im_l.append(fim.reshape(N_CTX, 2, S5_GROUPS, S5_P))
        gdn_l.append(st_gdn)

    y_prompt = x_all[:T_CTX].reshape(N_CTX, L_CTX, D_MODEL)
    y_sample = x_all[T_CTX:].reshape(N_LAT, L_LAT, D_MODEL)
    return (y_prompt, y_sample, jnp.stack(ret_l, 1), jnp.stack(s5re_l, 1), jnp.stack(s5im_l, 1),
            jnp.stack(gdn_l, 1))
```

```python
import functools

import jax
import jax.numpy as jnp
import numpy as np
from jax import lax
from jax.experimental import pallas as pl
from jax.experimental.pallas import tpu as pltpu

F32 = jnp.float32
BF16 = jnp.bfloat16
HIGHEST = lax.Precision.HIGHEST

D_MODEL = 1024
DEPTH = 4
N_CTX, L_CTX = 32, 256
N_LAT, L_LAT = 4, 1024
T_CTX = N_CTX * L_CTX
T_LAT = N_LAT * L_LAT
T_ALL = T_CTX + T_LAT
GRID_W = 64
CHUNK = 64
ROPE_BASE = 10000.0
ALPHA = (2 * DEPTH) ** 0.25
LN_EPS = 1e-6
RET_HEADS, RET_DK = 4, 64
RET_WIDTH = 256
S5_WIDTH, S5_CH, S5_GROUPS, S5_P = 256, 16, 16, 64
S5_STATE = S5_GROUPS * S5_P
GDN_HEADS, GDN_DK = 8, 64
GDN_WIDTH = 512
N_EXPERTS, EXPERTS_PER_GROUP, N_EGROUPS = 16, 4, 4
D_FF = 512
MAIN_COLS = 3328
LANES = 128
TOKEN_TILE = 512
N_TILES = T_ALL // TOKEN_TILE
CTX_TILES = T_CTX // TOKEN_TILE
TILES_PER_LAT = L_LAT // TOKEN_TILE
CTX_MOD_ROW = N_LAT
VMEM_LIMIT = 56 * 1024 * 1024


def _params(*sem):
    return pltpu.CompilerParams(dimension_semantics=sem, vmem_limit_bytes=VMEM_LIMIT)


def _ln(x):
    mu = jnp.mean(x, -1, keepdims=True)
    xc = x - mu
    var = jnp.mean(xc * xc, -1, keepdims=True)
    return xc * lax.rsqrt(var + LN_EPS)


def _silu(x):
    return x * jax.nn.sigmoid(x)


def _dot(a, b):
    return jnp.dot(a, b, preferred_element_type=F32)


def _dot_f32(a, b):
    return jnp.dot(a, b, precision=HIGHEST, preferred_element_type=F32)


def _dot_nt(a, b):
    return lax.dot_general(a, b, (((1,), (1,)), ((), ())), preferred_element_type=F32)


def _dot_tn(a, b):
    return lax.dot_general(a, b, (((0,), (0,)), ((), ())), preferred_element_type=F32)


def _mod_row(i):
    return jnp.where(i < CTX_TILES, CTX_MOD_ROW, (i - CTX_TILES) // TILES_PER_LAT)


ADA_TN = 1536


def _ada_kernel(c_ref, w_ref, b_ref, o_ref):
    o_ref[...] = _dot_f32(_silu(c_ref[...]), w_ref[...]) + b_ref[...]


def _ada_table(cond8, w_ada, b_ada):
    return pl.pallas_call(
        _ada_kernel,
        grid=(DEPTH, 6 * D_MODEL // ADA_TN),
        in_specs=[
            pl.BlockSpec((8, D_MODEL), lambda l, j: (0, 0)),
            pl.BlockSpec((None, D_MODEL, ADA_TN), lambda l, j: (l, 0, j)),
            pl.BlockSpec((None, 1, ADA_TN), lambda l, j: (l, 0, j)),
        ],
        out_specs=pl.BlockSpec((None, 8, ADA_TN), lambda l, j: (l, 0, j)),
        out_shape=jax.ShapeDtypeStruct((DEPTH, 8, 6 * D_MODEL), F32),
        compiler_params=_params("arbitrary", "arbitrary"),
        name="ada_table",
    )(cond8, w_ada, b_ada.reshape(DEPTH, 1, 6 * D_MODEL))


def _inproj_kernel(x_ref, mod_ref, w_ref, wab_ref, ret_ref, s5_ref, gdn_ref, ab_ref):
    mod = mod_ref[...]
    sh1 = mod[:, 0:D_MODEL]
    sc1 = mod[:, D_MODEL:2 * D_MODEL]
    h = (_ln(x_ref[...]) * (1.0 + sc1) + sh1).astype(BF16)
    ret_ref[...] = _dot(h, w_ref[:, 0:1024])
    s5_ref[...] = _dot(h, w_ref[:, 1024:1280])
    gdn_ref[...] = _dot(h, w_ref[:, 1280:MAIN_COLS])
    ab_ref[...] = _dot(h, wab_ref[...])


def _in_proj(x_all, mod_l, w_main, w_ab, layer):
    tm = TOKEN_TILE
    return pl.pallas_call(
        _inproj_kernel,
        grid=(N_TILES,),
        in_specs=[
            pl.BlockSpec((tm, D_MODEL), lambda i: (i, 0)),
            pl.BlockSpec((None, 1, 6 * D_MODEL), lambda i: (_mod_row(i), 0, 0)),
            pl.BlockSpec((None, D_MODEL, MAIN_COLS), lambda i: (layer, 0, 0)),
            pl.BlockSpec((None, D_MODEL, LANES), lambda i: (layer, 0, 0)),
        ],
        out_specs=[
            pl.BlockSpec((tm, 1024), lambda i: (i, 0)),
            pl.BlockSpec((tm, 256), lambda i: (i, 0)),
            pl.BlockSpec((tm, 2048), lambda i: (i, 0)),
            pl.BlockSpec((tm, LANES), lambda i: (i, 0)),
        ],
        out_shape=[
            jax.ShapeDtypeStruct((T_ALL, 1024), F32),
            jax.ShapeDtypeStruct((T_ALL, 256), F32),
            jax.ShapeDtypeStruct((T_ALL, 2048), F32),
            jax.ShapeDtypeStruct((T_ALL, LANES), F32),
        ],
        compiler_params=_params("arbitrary"),
        name="in_proj",
    )(x_all, mod_l, w_main, w_ab)


RET_TQ = 256


def _ret_kernel(*refs, L, rope, has_init, emit_state):
    it = iter(refs)
    x_ref, dec_ref = next(it), next(it)
    cos_ref = next(it) if rope else None
    sin_ref = next(it) if rope else None
    r0_ref = next(it) if has_init else None
    y_ref = next(it)
    st_ref = next(it) if emit_state else None

    lg = -jnp.exp(dec_ref[...])
    q = x_ref[:, 0:256]
    k = x_ref[:, 256:512] * (RET_DK ** -0.5)
    v = x_ref[:, 512:768]
    if rope:
        lane = lax.broadcasted_iota(jnp.int32, (L, RET_WIDTH), 1)
        first_half = (lane % RET_DK) < (RET_DK // 2)
        cos, sin = cos_ref[...], sin_ref[...]

        def rot(z):
            swapped = jnp.where(first_half, pltpu.roll(z, RET_WIDTH - RET_DK // 2, axis=1),
                                pltpu.roll(z, RET_DK // 2, axis=1))
            return z * cos + swapped * sin

        q, k = rot(q), rot(k)
    kb16 = k.astype(BF16)
    vb16 = v.astype(BF16)
    tcol_all = lax.broadcasted_iota(jnp.int32, (L, 1), 0).astype(F32)

    for qb in range(L // RET_TQ):
        rows = slice(qb * RET_TQ, (qb + 1) * RET_TQ)
        t_i = lax.broadcasted_iota(jnp.int32, (RET_TQ, L), 0) + qb * RET_TQ
        s_i = lax.broadcasted_iota(jnp.int32, (RET_TQ, L), 1)
        dist = (t_i - s_i).astype(F32)
        diag = jnp.where(t_i == s_i, 1.0, 0.0)
        tcol = tcol_all[rows]
        pieces = []
        for h in range(RET_HEADS):
            cols = slice(h * RET_DK, (h + 1) * RET_DK)
            lgf = lg[0:1, h:h + 1]
            lgb = lg[1:2, h:h + 1]
            qh = q[rows, cols]
            scores = _dot_nt(qh.astype(BF16), kb16[:, cols])
            decay = jnp.exp(jnp.where(dist >= 0, lgf * dist, -lgb * dist)) + diag
            o = _dot((scores * decay).astype(BF16), vb16[:, cols])
            if has_init:
                qf = (qh * jnp.exp(lgf * (tcol + 1.0))).astype(BF16)
                qr = (qh * jnp.exp(lgb * (float(L) - tcol))).astype(BF16)
                o = o + _dot(qf, r0_ref[0, h].astype(BF16)) + _dot(qr, r0_ref[1, h].astype(BF16))
            pieces.append(_ln(o) * _silu(x_ref[rows, 768 + h * RET_DK:768 + (h + 1) * RET_DK]))
        y_ref[rows, :] = jnp.concatenate(pieces, axis=1).astype(BF16)

    if emit_state:
        for h in range(RET_HEADS):
            cols = slice(h * RET_DK, (h + 1) * RET_DK)
            wf = jnp.exp(lg[0:1, h:h + 1] * (float(L - 1) - tcol_all))
            wb = jnp.exp(lg[1:2, h:h + 1] * tcol_all)
            st_ref[0, h] = _dot_tn((k[:, cols] * wf).astype(BF16), vb16[:, cols])
            st_ref[1, h] = _dot_tn((k[:, cols] * wb).astype(BF16), vb16[:, cols])


def _retention(ret_in, ret_decay_l, layer, *, latent, rope_tabs=None, state=None):
    L = L_LAT if latent else L_CTX
    nb = N_LAT if latent else N_CTX
    off = T_CTX // L if latent else 0
    kern = functools.partial(_ret_kernel, L=L, rope=latent, has_init=latent, emit_state=not latent)
    in_specs = [pl.BlockSpec((L, 1024), lambda b: (b + off, 0)),
                pl.BlockSpec((2, RET_HEADS), lambda b: (0, 0))]
    args = [ret_in, ret_decay_l]
    if latent:
        in_specs += [pl.BlockSpec((L, RET_WIDTH), lambda b: (0, 0))] * 2
        args += list(rope_tabs)
        in_specs.append(pl.BlockSpec((None, None, 2, RET_HEADS, RET_DK, RET_DK),
                                     lambda b: (b, layer, 0, 0, 0, 0)))
        args.append(state)
    out_specs = [pl.BlockSpec((L, RET_WIDTH), lambda b: (b, 0))]
    out_shape = [jax.ShapeDtypeStruct((nb * L, RET_WIDTH), BF16)]
    if not latent:
        out_specs.append(pl.BlockSpec((None, 2, RET_HEADS, RET_DK, RET_DK), lambda b: (b, 0, 0, 0, 0)))
        out_shape.append(jax.ShapeDtypeStruct((nb, 2, RET_HEADS, RET_DK, RET_DK), F32))
    return pl.pallas_call(
        kern, grid=(nb,), in_specs=in_specs, out_specs=out_specs, out_shape=out_shape,
        compiler_params=_params("arbitrary"),
        name="retention_lat" if latent else "retention_ctx",
    )(*args)


def _s5_prep_kernel(are_ref, aim_ref, ldt_ref, bre_ref, bim_ref, abar_ref, wb_ref):
    are, aim = are_ref[...], aim_ref[...]
    dt = jnp.exp(ldt_ref[...])
    mag = jnp.exp(are * dt)
    abr = mag * jnp.cos(aim * dt)
    abi = mag * jnp.sin(aim * dt)
    den = are * are + aim * aim
    cr = ((abr - 1.0) * are + abi * aim) / den
    ci = (abi * are - (abr - 1.0) * aim) / den
    bre, bim = bre_ref[...], bim_ref[...]
    for d in range(2):
        crd, cid = cr[d:d + 1], ci[d:d + 1]
        wb_ref[d, :, 0:S5_STATE] = (crd * bre - cid * bim).astype(BF16)
        wb_ref[d, :, S5_STATE:2 * S5_STATE] = (crd * bim + cid * bre).astype(BF16)
        abar_ref[2 * d:2 * d + 1, :] = abr[d:d + 1]
        abar_ref[2 * d + 1:2 * d + 2, :] = abi[d:d + 1]


def _s5_prep(a_re, a_im, ldt_rep, bre_bd, bim_bd):
    vec = pl.BlockSpec((None, 2, S5_STATE), lambda l: (l, 0, 0))
    mat = pl.BlockSpec((None, S5_WIDTH, S5_STATE), lambda l: (l, 0, 0))
    return pl.pallas_call(
        _s5_prep_kernel,
        grid=(DEPTH,),
        in_specs=[vec, vec, vec, mat, mat],
        out_specs=[pl.BlockSpec((None, 4, S5_STATE), lambda l: (l, 0, 0)),
                   pl.BlockSpec((None, 2, S5_WIDTH, 2 * S5_STATE), lambda l: (l, 0, 0, 0))],
        out_shape=[jax.ShapeDtypeStruct((DEPTH, 4, S5_STATE), F32),
                   jax.ShapeDtypeStruct((DEPTH, 2, S5_WIDTH, 2 * S5_STATE), BF16)],
        compiler_params=_params("arbitrary"),
        name="s5_prep",
    )(a_re, a_im, ldt_rep, bre_bd, bim_bd)


def _gelu_tanh(x):
    return x * (0.5 * (1.0 + jnp.tanh(np.sqrt(2.0 / np.pi).astype(np.float32) * (x + 0.044715 * (x * x * x)))))


def _s5_kernel(*refs, L, has_init, emit_state):
    it = iter(refs)
    u_ref, abar_ref, wb_ref, wcre_ref, wcim_ref, d_ref, wglu_ref = (next(it) for _ in range(7))
    x0re_ref = next(it) if has_init else None
    x0im_ref = next(it) if has_init else None
    y_ref = next(it)
    fre_ref = next(it) if emit_state else None
    fim_ref = next(it) if emit_state else None
    sf_ref, sb_ref = next(it), next(it)

    u = u_ref[...]
    ub = u.astype(BF16)
    sf_ref[...] = _dot(ub, wb_ref[0])
    sb_ref[...] = _dot(ub, wb_ref[1])
    afr, afi = abar_ref[0:1, :], abar_ref[1:2, :]
    abr, abi = abar_ref[2:3, :], abar_ref[3:4, :]
    if has_init:
        init = (x0re_ref[0:1, :], x0im_ref[0:1, :], x0re_ref[1:2, :], x0im_ref[1:2, :])
    else:
        init = tuple(jnp.zeros((1, S5_STATE), F32) for _ in range(4))

    def body(t, carry):
        xfr, xfi, xbr, xbi = carry
        tb = L - 1 - t
        nfr = afr * xfr - afi * xfi + sf_ref[pl.ds(t, 1), 0:S5_STATE]
        nfi = afr * xfi + afi * xfr + sf_ref[pl.ds(t, 1), S5_STATE:2 * S5_STATE]
        nbr = abr * xbr - abi * xbi + sb_ref[pl.ds(tb, 1), 0:S5_STATE]
        nbi = abr * xbi + abi * xbr + sb_ref[pl.ds(tb, 1), S5_STATE:2 * S5_STATE]
        sf_ref[pl.ds(t, 1), 0:S5_STATE] = nfr
        sf_ref[pl.ds(t, 1), S5_STATE:2 * S5_STATE] = nfi
        sb_ref[pl.ds(tb, 1), 0:S5_STATE] = nbr
        sb_ref[pl.ds(tb, 1), S5_STATE:2 * S5_STATE] = nbi
        return nfr, nfi, nbr, nbi

    xfr, xfi, xbr, xbi = lax.fori_loop(0, L, body, init, unroll=4)
    if emit_state:
        fre_ref[0:1, :] = xfr
        fre_ref[1:2, :] = xbr
        fim_ref[0:1, :] = xfi
        fim_ref[1:2, :] = xbi

    states = (sf_ref[...] + sb_ref[...]).astype(BF16)
    y = _dot(states[:, 0:S5_STATE], wcre_ref[...]) - _dot(states[:, S5_STATE:2 * S5_STATE], wcim_ref[...])
    y = _gelu_tanh(y + d_ref[...] * u)
    y_ref[...] = (y * jax.nn.sigmoid(_dot(y.astype(BF16), wglu_ref[...]))).astype(BF16)


def _s5(s5_in, abar_l, wb_l, wcre_l, wcim_l, d_l, wglu_l, layer, *, latent, x0re=None, x0im=None):
    L = L_LAT if latent else L_CTX
    nb = N_LAT if latent else N_CTX
    off = T_CTX // L if latent else 0
    kern = functools.partial(_s5_kernel, L=L, has_init=latent, emit_state=not latent)
    const2 = lambda b: (0, 0)
    in_specs = [pl.BlockSpec((L, S5_WIDTH), lambda b: (b + off, 0)),
                pl.BlockSpec((4, S5_STATE), const2),
                pl.BlockSpec((2, S5_WIDTH, 2 * S5_STATE), lambda b: (0, 0, 0)),
                pl.BlockSpec((S5_STATE, S5_WIDTH), const2),
                pl.BlockSpec((S5_STATE, S5_WIDTH), const2),
                pl.BlockSpec((1, S5_WIDTH), const2),
                pl.BlockSpec((S5_WIDTH, S5_WIDTH), const2)]
    args = [s5_in, abar_l, wb_l, wcre_l, wcim_l, d_l, wglu_l]
    if latent:
        st = pl.BlockSpec((None, None, 2, S5_STATE), lambda b: (b, layer, 0, 0))
        in_specs += [st, st]
        args += [x0re, x0im]
    out_specs = [pl.BlockSpec((L, S5_WIDTH), lambda b: (b, 0))]
    out_shape = [jax.ShapeDtypeStruct((nb * L, S5_WIDTH), BF16)]
    if not latent:
        fs = pl.BlockSpec((None, 2, S5_STATE), lambda b: (b, 0, 0))
        out_specs += [fs, fs]
        out_shape += [jax.ShapeDtypeStruct((nb, 2, S5_STATE), F32)] * 2
    return pl.pallas_call(
        kern, grid=(nb,), in_specs=in_specs, out_specs=out_specs, out_shape=out_shape,
        scratch_shapes=[pltpu.VMEM((L, 2 * S5_STATE), F32), pltpu.VMEM((L, 2 * S5_STATE), F32)],
        compiler_params=_params("arbitrary"),
        name="s5_lat" if latent else "s5_ctx",
    )(*args)


def _softplus(x):
    return jnp.maximum(x, 0.0) + jnp.log1p(jnp.exp(-jnp.abs(x)))


def _split(x):
    hi = x.astype(BF16)
    return hi, (x - hi.astype(F32)).astype(BF16)


GDN_PACK = 4
PACK_W = GDN_PACK * GDN_DK
N_PACKS = GDN_HEADS // GDN_PACK


def _split3(x):
    hi = x.astype(BF16)
    r = x - hi.astype(F32)
    mid = r.astype(BF16)
    return hi, mid, (r - mid.astype(F32)).astype(BF16)


def _dot_exact_lhs(a16, x):
    hi, mid, lo = _split3(x)
    return _dot(a16, hi) + _dot(a16, mid) + _dot(a16, lo)


def _gdn_packed_kernel(*refs, L, has_init, emit_state):
    it = iter(refs)
    x_ref, ab_ref, conv_ref, alog_ref, dtb_ref, norm_ref = (next(it) for _ in range(6))
    s0_ref = next(it) if has_init else None
    y_ref = next(it)
    sfin_ref = next(it) if emit_state else None
    q_s, k_s, v_s, gfx_s, gbx_s, betax_s, of_s, ob_s, st_s = (next(it) for _ in range(9))
    n = L // CHUNK
    qkv = 3 * GDN_WIDTH

    w0, w1, w2 = conv_ref[0:1, :], conv_ref[1:2, :], conv_ref[2:3, :]
    rate_f, rate_b = jnp.exp(alog_ref[0:1, :]), jnp.exp(alog_ref[1:2, :])
    dtb_f, dtb_b = dtb_ref[0:1, :], dtb_ref[1:2, :]
    row_q = lax.broadcasted_iota(jnp.int32, (CHUNK, qkv), 0)
    e_row = lax.broadcasted_iota(jnp.int32, (LANES, 2 * GDN_WIDTH), 0)
    e_col = lax.broadcasted_iota(jnp.int32, (LANES, 2 * GDN_WIDTH), 1)
    expand = jnp.where(e_row == jnp.where(e_col < GDN_WIDTH, e_col >> 6, GDN_HEADS + ((e_col - GDN_WIDTH) >> 6)),
                       1.0, 0.0).astype(BF16)

    def prep(c, _):
        r0 = pl.multiple_of(c * CHUNK, CHUNK)
        xc = x_ref[pl.ds(r0, CHUNK), 0:qkv]
        prev = x_ref[pl.ds(jnp.maximum(r0 - 1, 0), 1), 0:qkv] * (c > 0).astype(F32)
        nxt = x_ref[pl.ds(jnp.minimum(r0 + CHUNK, L - 1), 1), 0:qkv] * (c < n - 1).astype(F32)
        xm1 = jnp.where(row_q == 0, prev, pltpu.roll(xc, 1, axis=0))
        xp1 = jnp.where(row_q == CHUNK - 1, nxt, pltpu.roll(xc, CHUNK - 1, axis=0))
        y = _silu(w0 * xm1 + w1 * xc + w2 * xp1)
        qs, ks = [], []
        for h in range(GDN_HEADS):
            qh = y[:, h * GDN_DK:(h + 1) * GDN_DK]
            kh = y[:, GDN_WIDTH + h * GDN_DK:GDN_WIDTH + (h + 1) * GDN_DK]
            qs.append(qh * lax.rsqrt(jnp.sum(qh * qh, -1, keepdims=True) + 1e-6) * (GDN_DK ** -0.5))
            ks.append(kh * lax.rsqrt(jnp.sum(kh * kh, -1, keepdims=True) + 1e-6))
        q_s[pl.ds(r0, CHUNK), :] = jnp.concatenate(qs, axis=1)
        k_s[pl.ds(r0, CHUNK), :] = jnp.concatenate(ks, axis=1)
        v_s[pl.ds(r0, CHUNK), :] = y[:, 2 * GDN_WIDTH:qkv]
        ab = ab_ref[pl.ds(r0, CHUNK), :]
        gates = jnp.concatenate([-rate_f * _softplus(ab + dtb_f), -rate_b * _softplus(ab + dtb_b),
                                 jax.nn.sigmoid(ab)], axis=0)
        hi, mid, lo = _split3(gates)
        wide = _dot(hi, expand) + _dot(mid, expand) + _dot(lo, expand)
        gfx_s[pl.ds(r0, CHUNK), :] = wide[0:CHUNK, 0:GDN_WIDTH]
        gbx_s[pl.ds(r0, CHUNK), :] = wide[CHUNK:2 * CHUNK, 0:GDN_WIDTH]
        betax_s[pl.ds(r0, CHUNK), :] = wide[2 * CHUNK:3 * CHUNK, GDN_WIDTH:2 * GDN_WIDTH]
        return 0

    lax.fori_loop(0, n, prep, 0)

    for d in range(2):
        for h in range(GDN_HEADS):
            cols = slice(h * GDN_DK, (h + 1) * GDN_DK)
            st_s[d, :, cols] = s0_ref[d, h] if has_init else jnp.zeros((GDN_DK, GDN_DK), F32)

    row_w = lax.broadcasted_iota(jnp.int32, (CHUNK, GDN_WIDTH), 0)
    lane_w = lax.broadcasted_iota(jnp.int32, (CHUNK, GDN_WIDTH), 1)
    col_w = lane_w & (GDN_DK - 1)
    eye_w = jnp.where(row_w == col_w, 1.0, 0.0)
    row_p = lax.broadcasted_iota(jnp.int32, (CHUNK, PACK_W), 0)
    lane_p = lax.broadcasted_iota(jnp.int32, (CHUNK, PACK_W), 1)
    eye_p = jnp.where(row_p == (lane_p & (GDN_DK - 1)), 1.0, 0.0)
    head_p = lane_p >> 6
    bd_row = lax.broadcasted_iota(jnp.int32, (PACK_W, PACK_W), 0) >> 6
    bd_col = lax.broadcasted_iota(jnp.int32, (PACK_W, PACK_W), 1) >> 6
    bd_mask = bd_row == bd_col
    r64 = lax.broadcasted_iota(jnp.int32, (CHUNK, CHUNK), 0)
    c64 = lax.broadcasted_iota(jnp.int32, (CHUNK, CHUNK), 1)
    ones16 = jnp.ones((CHUNK, CHUNK), BF16)
    zero16 = jnp.zeros((), BF16)

    def bd(x16):
        return jnp.where(bd_mask, jnp.concatenate([x16] * GDN_PACK, axis=0), zero16)

    def step(i, _):
        chains = []
        for d in range(2):
            c = i if d == 0 else n - 1 - i
            r0 = pl.multiple_of(c * CHUNK, CHUNK)
            incl64 = (r64 >= c64) if d == 0 else (r64 <= c64)
            incl = (row_w >= col_w) if d == 0 else (row_w <= col_w)
            strict = (row_w > col_w) if d == 0 else (row_w < col_w)
            last = CHUNK - 1 if d == 0 else 0
            gx = (gfx_s if d == 0 else gbx_s)[pl.ds(r0, CHUNK), :]
            gc = _dot_exact_lhs(jnp.where(incl64, 1.0, 0.0).astype(BF16), gx)
            grow = _dot_exact_lhs(ones16, gc * eye_w)
            dmat = jnp.where(incl, jnp.exp(jnp.where(incl, gc - grow, 0.0)), 0.0)
            eg = jnp.exp(gc)
            glast = gc[last:last + 1, :]
            beta = betax_s[pl.ds(r0, CHUNK), :]
            qc = q_s[pl.ds(r0, CHUNK), :]
            kc = k_s[pl.ds(r0, CHUNK), :]
            vc = v_s[pl.ds(r0, CHUNK), :]
            kbeta = kc * beta
            vbeta = vc * beta
            k_dec = kc * jnp.exp(glast - gc)
            s_decay = jnp.exp(glast)
            s_all = st_s[d]
            for p in range(N_PACKS):
                cols = slice(p * PACK_W, (p + 1) * PACK_W)
                chains.append(dict(
                    d=d, p_idx=p, incl=incl[:, cols], strict=strict[:, cols], dmat=dmat[:, cols], q=qc[:, cols],
                    k=kc[:, cols], kbeta=kbeta[:, cols], vbeta=vbeta[:, cols], eg=eg[:, cols],
                    k_dec=k_dec[:, cols], s_decay=s_decay[:, cols], s=s_all[:, cols]))

        for ch in chains:
            lhs = jnp.concatenate([ch["kbeta"], ch["q"]], axis=0).astype(BF16)
            both = _dot_nt(lhs, bd(ch["k"].astype(BF16)))
            ch["p"] = -jnp.where(ch["strict"], both[0:CHUNK] * ch["dmat"], 0.0)
            ch["attn"] = jnp.where(ch["incl"], both[CHUNK:2 * CHUNK] * ch["dmat"], 0.0)
            ch["t"] = eye_p + ch["p"]

        for stage in range(6):
            for ch in chains:
                p_hi, p_lo = _split(ch["p"])
                rhs_hi, rhs_lo = bd(p_hi), bd(p_lo)
                parts_hi, parts_lo = [], []
                if stage > 0:
                    t_hi, t_lo = _split(ch["t"])
                    parts_hi += [t_hi, t_lo]
                    parts_lo += [t_hi]
                if stage < 5:
                    parts_hi += [p_hi, p_lo]
                    parts_lo += [p_hi]
                out_hi = _dot(jnp.concatenate(parts_hi, axis=0), rhs_hi)
                out_lo = _dot(jnp.concatenate(parts_lo, axis=0) if len(parts_lo) > 1 else parts_lo[0], rhs_lo)
                at = 0
                if stage > 0:
                    ch["t"] = ch["t"] + (out_hi[0:CHUNK] + out_hi[CHUNK:2 * CHUNK] + out_lo[0:CHUNK])
                    at = 1
                if stage < 5:
                    ch["p"] = (out_hi[2 * at * CHUNK:(2 * at + 1) * CHUNK] + out_hi[(2 * at + 1) * CHUNK:(2 * at + 2) * CHUNK]
                               + out_lo[at * CHUNK:(at + 1) * CHUNK])

        outs = [[None] * N_PACKS, [None] * N_PACKS]
        new_s = [[None] * N_PACKS, [None] * N_PACKS]
        for ch in chains:
            lhs = jnp.concatenate([ch["kbeta"] * ch["eg"], ch["q"] * ch["eg"]], axis=0).astype(BF16)
            both = _dot(lhs, bd(ch["s"].astype(BF16)))
            r_hi, r_lo = _split(ch["vbeta"] - both[0:CHUNK])
            t_hi, t_lo = _split(ch["t"])
            tr = _dot(jnp.concatenate([t_hi, t_lo], axis=0), bd(r_hi))
            v_new = tr[0:CHUNK] + tr[CHUNK:2 * CHUNK] + _dot(t_hi, bd(r_lo))
            vn16 = v_new.astype(BF16)
            outs[ch["d"]][ch["p_idx"]] = both[CHUNK:2 * CHUNK] + _dot(ch["attn"].astype(BF16), bd(vn16))
            upd = _dot_tn(ch["k_dec"].astype(BF16), vn16)
            s_new = ch["s"] * ch["s_decay"]
            for h in range(GDN_PACK):
                s_new = s_new + jnp.where(head_p == h, upd[h * GDN_DK:(h + 1) * GDN_DK, :], 0.0)
            new_s[ch["d"]][ch["p_idx"]] = s_new

        for d in range(2):
            st_s[d] = jnp.concatenate(new_s[d], axis=1)
        of_s[pl.ds(pl.multiple_of(i * CHUNK, CHUNK), CHUNK), :] = jnp.concatenate(outs[0], axis=1)
        ob_s[pl.ds(pl.multiple_of((n - 1 - i) * CHUNK, CHUNK), CHUNK), :] = jnp.concatenate(outs[1], axis=1)
        return 0

    lax.fori_loop(0, n, step, 0)
    if emit_state:
        for d in range(2):
            for h in range(GDN_HEADS):
                sfin_ref[d, h] = st_s[d, :, h * GDN_DK:(h + 1) * GDN_DK]

    gain = norm_ref[...]

    def fin(c, _):
        r0 = pl.multiple_of(c * CHUNK, CHUNK)
        o = of_s[pl.ds(r0, CHUNK), :] + ob_s[pl.ds(r0, CHUNK), :]
        z = x_ref[pl.ds(r0, CHUNK), qkv:qkv + GDN_WIDTH]
        pieces = []
        for h in range(GDN_HEADS):
            cols = slice(h * GDN_DK, (h + 1) * GDN_DK)
            oh = o[:, cols]
            on = oh * lax.rsqrt(jnp.mean(oh * oh, -1, keepdims=True) + LN_EPS) * gain
            pieces.append(on * _silu(z[:, cols]))
        y_ref[pl.ds(r0, CHUNK), :] = jnp.concatenate(pieces, axis=1).astype(BF16)
        return 0

    lax.fori_loop(0, n, fin, 0)


def _gdn(gdn_in, ab_in, conv_l, alog_l, dtb_l, norm_l, layer, *, latent, state=None):
    L = L_LAT if latent else L_CTX
    nb = N_LAT if latent else N_CTX
    off = T_CTX // L if latent else 0
    kern = functools.partial(_gdn_packed_kernel, L=L, has_init=latent, emit_state=not latent)
    const2 = lambda b: (0, 0)
    in_specs = [pl.BlockSpec((L, 2048), lambda b: (b + off, 0)),
                pl.BlockSpec((L, LANES), lambda b: (b + off, 0)),
                pl.BlockSpec((3, 3 * GDN_WIDTH), const2),
                pl.BlockSpec((2, LANES), const2),
                pl.BlockSpec((2, LANES), const2),
                pl.BlockSpec((1, GDN_DK), const2)]
    args = [gdn_in, ab_in, conv_l, alog_l, dtb_l, norm_l]
    if latent:
        in_specs.append(pl.BlockSpec((None, None, 2, GDN_HEADS, GDN_DK, GDN_DK),
                                     lambda b: (b, layer, 0, 0, 0, 0)))
        args.append(state)
    out_specs = [pl.BlockSpec((L, GDN_WIDTH), lambda b: (b, 0))]
    out_shape = [jax.ShapeDtypeStruct((nb * L, GDN_WIDTH), BF16)]
    if not latent:
        out_specs.append(pl.BlockSpec((None, 2, GDN_HEADS, GDN_DK, GDN_DK), lambda b: (b, 0, 0, 0, 0)))
        out_shape.append(jax.ShapeDtypeStruct((nb, 2, GDN_HEADS, GDN_DK, GDN_DK), F32))
    wide = pltpu.VMEM((L, GDN_WIDTH), F32)
    return pl.pallas_call(
        kern, grid=(nb,), in_specs=in_specs, out_specs=out_specs, out_shape=out_shape,
        scratch_shapes=[wide] * 8 + [pltpu.VMEM((2, GDN_DK, GDN_WIDTH), F32)],
        compiler_params=_params("arbitrary"),
        name="gdn_lat" if latent else "gdn_ctx",
    )(*args)


def _post_kernel(x_ref, yrc_ref, yrl_ref, ysc_ref, ysl_ref, ygc_ref, ygl_ref, mod_ref, wout_ref, g_ref, b_ref,
                 wr_ref, br_ref, x1_ref, h2_ref, gate_ref):
    is_ctx = pl.program_id(0) < CTX_TILES
    yr = jnp.where(is_ctx, yrc_ref[...], yrl_ref[...])
    ys = jnp.where(is_ctx, ysc_ref[...], ysl_ref[...])
    yg = jnp.where(is_ctx, ygc_ref[...], ygl_ref[...])
    mix = _dot(yr, wout_ref[0:256, :]) + _dot(ys, wout_ref[256:512, :]) + _dot(yg, wout_ref[512:1024, :])
    mod = mod_ref[...]
    g1 = mod[:, 2 * D_MODEL:3 * D_MODEL]
    sh2 = mod[:, 3 * D_MODEL:4 * D_MODEL]
    sc2 = mod[:, 4 * D_MODEL:5 * D_MODEL]
    x1 = _ln(ALPHA * x_ref[...] + g1 * mix) * g_ref[...] + b_ref[...]
    x1_ref[...] = x1
    h2 = _ln(x1) * (1.0 + sc2) + sh2
    h2_ref[...] = h2.astype(BF16)

    h_hi, h_lo = _split(h2)
    w_hi, w_lo = _split(wr_ref[...])
    logits = _dot(h_hi, w_hi) + _dot(h_hi, w_lo) + _dot(h_lo, w_hi) + br_ref[...]
    lane = lax.broadcasted_iota(jnp.int32, logits.shape, 1)
    lane_f = lane.astype(F32)
    big = float(LANES)
    neg = -jnp.inf
    is_g = (lane >= N_EXPERTS) & (lane < N_EXPERTS + N_EGROUPS)
    gl = jnp.where(is_g, logits, neg)
    gmax = jnp.max(gl, -1, keepdims=True)
    g_sel = jnp.min(jnp.where(is_g & (gl == gmax), lane_f, big), -1, keepdims=True) - float(N_EXPERTS)
    p_group = 1.0 / jnp.sum(jnp.where(is_g, jnp.exp(gl - gmax), 0.0), -1, keepdims=True)
    in_sel = (lane < N_EXPERTS) & ((lane >> 2).astype(F32) == g_sel)
    el = jnp.where(in_sel, logits, neg)
    emax = jnp.max(el, -1, keepdims=True)
    ee = jnp.where(in_sel, jnp.exp(el - emax), 0.0)
    prob = ee / jnp.sum(ee, -1, keepdims=True)
    p1 = jnp.max(jnp.where(in_sel, prob, -1.0), -1, keepdims=True)
    i1 = jnp.min(jnp.where(in_sel & (prob == p1), lane_f, big), -1, keepdims=True)
    rest = in_sel & (lane_f != i1)
    p2 = jnp.max(jnp.where(rest, prob, -1.0), -1, keepdims=True)
    i2 = jnp.min(jnp.where(rest & (prob == p2), lane_f, big), -1, keepdims=True)
    den = p1 + p2
    gate_ref[...] = jnp.where(lane_f == i1, p_group * p1 / den,
                              jnp.where(lane_f == i2, p_group * p2 / den, 0.0))


def _post(x_all, y_ret, y_s5, y_gdn, mod_l, w_out16, layer, ln_g, ln_b, wr_l, br_l):
    tm = TOKEN_TILE
    const2 = lambda i: (0, 0)
    ctx_tile = lambda i: (jnp.minimum(i, CTX_TILES - 1), 0)
    lat_tile = lambda i: (jnp.maximum(i - CTX_TILES, 0), 0)
    pair = lambda w: [pl.BlockSpec((tm, w), ctx_tile), pl.BlockSpec((tm, w), lat_tile)]
    return pl.pallas_call(
        _post_kernel,
        grid=(N_TILES,),
        in_specs=[
            pl.BlockSpec((tm, D_MODEL), lambda i: (i, 0)),
            *pair(RET_WIDTH), *pair(S5_WIDTH), *pair(GDN_WIDTH),
            pl.BlockSpec((None, 1, 6 * D_MODEL), lambda i: (_mod_row(i), 0, 0)),
            pl.BlockSpec((None, D_MODEL, D_MODEL), lambda i: (layer, 0, 0)),
            pl.BlockSpec((1, D_MODEL), const2),
            pl.BlockSpec((1, D_MODEL), const2),
            pl.BlockSpec((D_MODEL, LANES), const2),
            pl.BlockSpec((1, LANES), const2),
        ],
        out_specs=[
            pl.BlockSpec((tm, D_MODEL), lambda i: (i, 0)),
            pl.BlockSpec((tm, D_MODEL), lambda i: (i, 0)),
            pl.BlockSpec((tm, LANES), lambda i: (i, 0)),
        ],
        out_shape=[
            jax.ShapeDtypeStruct((T_ALL, D_MODEL), F32),
            jax.ShapeDtypeStruct((T_ALL, D_MODEL), BF16),
            jax.ShapeDtypeStruct((T_ALL, LANES), F32),
        ],
        compiler_params=_params("arbitrary"),
        name="post_mix",
    )(x_all, *y_ret, *y_s5, *y_gdn, mod_l, w_out16, ln_g, ln_b, wr_l, br_l)


def _moe_kernel(h_ref, gate_ref, x1_ref, mod_ref, wg_ref, wu_ref, wd_ref, g_ref, b_ref, o_ref, acc_ref):
    e = pl.program_id(1)

    @pl.when(e == 0)
    def _():
        acc_ref[...] = jnp.zeros(acc_ref.shape, F32)

    h = h_ref[...]
    gate = gate_ref[...]
    lane = lax.broadcasted_iota(jnp.int32, gate.shape, 1)
    ge = jnp.sum(jnp.where(lane == e, gate, 0.0), -1, keepdims=True)
    act = _silu(_dot(h, wg_ref[...])) * _dot(h, wu_ref[...]) * ge
    acc_ref[...] += _dot(act.astype(BF16), wd_ref[...])

    @pl.when(e == N_EXPERTS - 1)
    def _():
        g2 = mod_ref[:, 5 * D_MODEL:6 * D_MODEL]
        o_ref[...] = _ln(ALPHA * x1_ref[...] + g2 * acc_ref[...]) * g_ref[...] + b_ref[...]


def _moe(h2, gate, x1, mod_l, wg16, wu16, wd16, layer, ln_g, ln_b):
    tm = TOKEN_TILE
    const2 = lambda i, e: (0, 0)
    expert = lambda i, e: (layer, e, 0, 0)
    return pl.pallas_call(
        _moe_kernel,
        grid=(N_TILES, N_EXPERTS),
        in_specs=[
            pl.BlockSpec((tm, D_MODEL), lambda i, e: (i, 0)),
            pl.BlockSpec((tm, LANES), lambda i, e: (i, 0)),
            pl.BlockSpec((tm, D_MODEL), lambda i, e: (i, 0)),
            pl.BlockSpec((None, 1, 6 * D_MODEL), lambda i, e: (_mod_row(i), 0, 0)),
            pl.BlockSpec((None, None, D_MODEL, D_FF), expert),
            pl.BlockSpec((None, None, D_MODEL, D_FF), expert),
            pl.BlockSpec((None, None, D_FF, D_MODEL), expert),
            pl.BlockSpec((1, D_MODEL), const2),
            pl.BlockSpec((1, D_MODEL), const2),
        ],
        out_specs=pl.BlockSpec((tm, D_MODEL), lambda i, e: (i, 0)),
        out_shape=jax.ShapeDtypeStruct((T_ALL, D_MODEL), F32),
        scratch_shapes=[pltpu.VMEM((tm, D_MODEL), F32)],
        compiler_params=_params("arbitrary", "arbitrary"),
        name="moe",
    )(h2, gate, x1, mod_l, wg16, wu16, wd16, ln_g, ln_b)


def _rope_tables():
    rows = L_LAT // GRID_W
    row = jnp.repeat(jnp.arange(rows, dtype=F32), GRID_W)
    col = (jnp.arange(rows * GRID_W) % GRID_W).astype(F32)
    n_freq = RET_DK // 4
    inv_freq = ROPE_BASE ** (-jnp.arange(n_freq, dtype=F32) / n_freq)
    ang = jnp.concatenate([row[:, None] * inv_freq, col[:, None] * inv_freq], -1)
    cos, sin = jnp.cos(ang), jnp.sin(ang)
    cos_t = jnp.tile(jnp.concatenate([cos, cos], -1), (1, RET_HEADS))
    sin_t = jnp.tile(jnp.concatenate([-sin, sin], -1), (1, RET_HEADS))
    return cos_t, sin_t


def _block_diag_in(b):
    eye = jnp.eye(S5_GROUPS, dtype=F32)
    bt = jnp.transpose(b, (0, 1, 3, 2))
    return (bt[:, :, :, None, :] * eye[None, :, None, :, None]).reshape(DEPTH, S5_WIDTH, S5_STATE)


def _block_diag_out(c):
    eye = jnp.eye(S5_GROUPS, dtype=F32)
    ct = jnp.transpose(c, (0, 1, 3, 2))
    return (ct[:, :, :, None, :] * eye[None, :, None, :, None]).reshape(DEPTH, S5_STATE, S5_WIDTH)


def _pad_lanes(a):
    return jnp.pad(a, [(0, 0)] * (a.ndim - 1) + [(0, LANES - a.shape[-1])])


def kernel(x_prompt, x_sample, state_ret, state_s5_re, state_s5_im, state_gdn, c, c_ctx, w_ada, b_ada, w_in, ret_decay, s5_a_re, s5_a_im, s5_log_dt, s5_b_re, s5_b_im, s5_c_re, s5_c_im, s5_d, s5_w_glu, gdn_conv, gdn_a_log, gdn_dt_bias, gdn_norm, w_out, ln1_g, ln1_b, ln2_g, ln2_b, router_group_w, router_group_b, router_expert_w, router_expert_b, w_gate, w_up, w_down):
    x_all = jnp.concatenate([x_prompt.reshape(T_CTX, D_MODEL), x_sample.reshape(T_LAT, D_MODEL)], 0)
    cond8 = jnp.concatenate([c, c_ctx[None, :], jnp.zeros((8 - N_LAT - 1, D_MODEL), F32)], 0)
    mod = _ada_table(cond8, w_ada, b_ada).reshape(DEPTH, 8, 1, 6 * D_MODEL)

    w_main = w_in[:, :, :MAIN_COLS].astype(BF16)
    w_ab = _pad_lanes(w_in[:, :, MAIN_COLS:]).astype(BF16)
    w_out16 = w_out.astype(BF16)
    wg16, wu16, wd16 = w_gate.astype(BF16), w_up.astype(BF16), w_down.astype(BF16)
    wr = _pad_lanes(jnp.concatenate([router_expert_w, router_group_w], -1))
    br = _pad_lanes(jnp.concatenate([router_expert_b, router_group_b], -1)).reshape(DEPTH, 1, LANES)
    rope_tabs = _rope_tables()

    abar, wb = _s5_prep(s5_a_re.reshape(DEPTH, 2, S5_STATE), s5_a_im.reshape(DEPTH, 2, S5_STATE),
                        jnp.repeat(s5_log_dt, S5_P, axis=-1),
                        _block_diag_in(s5_b_re), _block_diag_in(s5_b_im))
    wcre = _block_diag_out(s5_c_re).astype(BF16)
    wcim = _block_diag_out(s5_c_im).astype(BF16)
    wglu16 = s5_w_glu.astype(BF16)
    x0re = state_s5_re.reshape(N_LAT, DEPTH, 2, S5_STATE)
    x0im = state_s5_im.reshape(N_LAT, DEPTH, 2, S5_STATE)
    alog = _pad_lanes(gdn_a_log)
    dtb = _pad_lanes(gdn_dt_bias)

    ret_l, s5re_l, s5im_l, gdn_l = [], [], [], []
    for l in range(DEPTH):
        ret_in, s5_in, gdn_in, ab_in = _in_proj(x_all, mod[l], w_main, w_ab, l)

        yr_c, st_ret = _retention(ret_in, ret_decay[l], l, latent=False)
        (yr_l,) = _retention(ret_in, ret_decay[l], l, latent=True, rope_tabs=rope_tabs, state=state_ret)
        s5_args = (abar[l], wb[l], wcre[l], wcim[l], s5_d[l].reshape(1, S5_WIDTH), wglu16[l], l)
        ys_c, fre, fim = _s5(s5_in, *s5_args, latent=False)
        (ys_l,) = _s5(s5_in, *s5_args, latent=True, x0re=x0re, x0im=x0im)
        gdn_args = (gdn_conv[l], alog[l], dtb[l], gdn_norm[l].reshape(1, GDN_DK), l)
        yg_c, st_gdn = _gdn(gdn_in, ab_in, *gdn_args, latent=False)
        (yg_l,) = _gdn(gdn_in, ab_in, *gdn_args, latent=True, state=state_gdn)

        x1, h2, gate = _post(x_all, (yr_c, yr_l), (ys_c, ys_l), (yg_c, yg_l), mod[l], w_out16, l,
                             ln1_g[l].reshape(1, D_MODEL), ln1_b[l].reshape(1, D_MODEL), wr[l], br[l])
        x_all = _moe(h2, gate, x1, mod[l], wg16, wu16, wd16, l,
                     ln2_g[l].reshape(1, D_MODEL), ln2_b[l].reshape(1, D_MODEL))

        ret_l.append(st_ret)
        s5re_l.append(fre.reshape(N_CTX, 2, S5_GROUPS, S5_P))
        s5im_l.append(fim.reshape(N_CTX, 2, S5_GROUPS, S5_P))
        gdn_l.append(st_gdn)

    y_prompt = x_all[:T_CTX].reshape(N_CTX, L_CTX, D_MODEL)
    y_sample = x_all[T_CTX:].reshape(N_LAT, L_LAT, D_MODEL)
    return (y_prompt, y_sample, jnp.stack(ret_l, 1), jnp.stack(s5re_l, 1), jnp.stack(s5im_l, 1),
            jnp.stack(gdn_l, 1))
```
